```python
import math, functools
import jax, jax.numpy as jnp
from jax import lax
import numpy as np

D_MODEL = 4096
BATCH = 2
SEQ = 8192
DEPTH = 2

GRID_W = 64
CTX_LEN = 256
D_MIX = D_MODEL
D_SSM = D_MIX // 2
D_CONF = D_MIX - D_SSM
SSM_HEADDIM = 64
SSM_HEADS = D_SSM // SSM_HEADDIM
SSM_GROUPS = 8
HEADS_PER_GROUP = SSM_HEADS // SSM_GROUPS
SSM_STATE = 128
SSM_CONV = 5
CHUNK = 128
D_XB = D_SSM + SSM_GROUPS * SSM_STATE
D_XBC = D_XB + SSM_GROUPS * SSM_STATE
DT_MIN = 1e-3
DT_MAX = 1e-1
A_MIN = 1.0
A_MAX = 16.0
CONF_WIDTH = 31
OFF_DT = D_SSM
OFF_X = OFF_DT + 2 * SSM_HEADS
OFF_C = OFF_X + D_XB
OFF_GLU = OFF_X + D_XBC
D_IN_PROJ = OFF_GLU + 2 * D_CONF
FFN_DENSE = 11008
N_EXPERTS = 8
TOP_K = 2
FFN_EXPERT = 3584
LN_EPS = 1e-5
ALPHA = (2 * DEPTH) ** 0.25
BETA = (8 * DEPTH) ** -0.25

kernel_name = "hybrid_ssd_conformer_moe_dit"


def _standardize(t):
    tf = t.astype(jnp.float32)
    mu = jnp.mean(tf, axis=-1, keepdims=True)
    var = jnp.mean(jnp.square(tf - mu), axis=-1, keepdims=True)
    return (tf - mu) * lax.rsqrt(var + LN_EPS)


def layer_norm(t, gain, bias):
    return (_standardize(t) * gain + bias).astype(t.dtype)


def modulate(t, shift, scale):
    return (_standardize(t) * (1 + scale) + shift).astype(t.dtype)


def dwconv1d(v, w, b):
    out = lax.conv_general_dilated(v, w[:, None, :], window_strides=(1,), padding="SAME",
                                   dimension_numbers=("NWC", "WIO", "NWC"),
                                   feature_group_count=v.shape[-1])
    return out + b


def dwconv_grid(v, w, b, rows, vertical):
    bsz, length, ch = v.shape
    img = v.reshape(bsz, rows, GRID_W, ch)
    taps = w[:, None, None, :] if vertical else w[None, :, None, :]
    out = lax.conv_general_dilated(img, taps, window_strides=(1, 1), padding="SAME",
                                   dimension_numbers=("NHWC", "HWIO", "NHWC"),
                                   feature_group_count=ch)
    return out.reshape(bsz, length, ch) + b


def ssm_conv(xbc_raw, w, b):
    return jax.nn.silu(dwconv1d(xbc_raw, w, b))


def flip(t):
    return jnp.flip(t, axis=1)


def to_chunks(t):
    return t.reshape(t.shape[0], t.shape[1] // CHUNK, CHUNK, *t.shape[2:])


def ssd_direction(xs, dt_raw, dt_bias, a_log):
    bsz, length = dt_raw.shape[:2]
    dt = jax.nn.softplus(dt_raw.astype(jnp.float32) + dt_bias.astype(jnp.float32))
    dt = dt.reshape(bsz, length, SSM_GROUPS, HEADS_PER_GROUP)
    a = -dt * jnp.exp(a_log.astype(jnp.float32)).reshape(SSM_GROUPS, HEADS_PER_GROUP)
    return xs * dt[..., None].astype(xs.dtype), a


def ssd_states(xc, ac, bc, init):
    a_cum = jnp.cumsum(ac, axis=2)
    a_tot = a_cum[:, :, -1]
    decay_to_end = jnp.exp(a_tot[:, :, None] - a_cum).astype(xc.dtype)
    chunk_states = jnp.einsum("bclgn,bclge,bclgep->bcgepn", bc, decay_to_end, xc)
    chunk_decay = jnp.exp(a_tot).astype(xc.dtype)

    def step(h, inp):
        s, d = inp
        return h * d[..., None, None] + s, h

    final, prev = lax.scan(step, init, (jnp.moveaxis(chunk_states, 1, 0),
                                        jnp.moveaxis(chunk_decay, 1, 0)))
    return a_cum, jnp.moveaxis(prev, 0, 1), final


def ssd_output(xc, bc, cc, a_cum, prev):
    seg = a_cum[:, :, :, None] - a_cum[:, :, None, :]
    lower = jnp.tril(jnp.ones((CHUNK, CHUNK), dtype=bool))[:, :, None, None]
    decay = jnp.exp(jnp.where(lower, seg, -jnp.inf)).astype(xc.dtype)
    cb = jnp.einsum("bclgn,bcsgn->bclsg", cc, bc)
    y_diag = jnp.einsum("bclsge,bcsgep->bclgep", cb[..., None] * decay, xc)
    y_off = jnp.einsum("bclgn,bcgepn->bclgep", cc, prev) * jnp.exp(a_cum).astype(xc.dtype)[..., None]
    return y_diag + y_off


def ssd_scan(xdt, a, bm, cm, init):
    xc, bc, cc = to_chunks(xdt), to_chunks(bm), to_chunks(cm)
    a_cum, prev, final = ssd_states(xc, to_chunks(a), bc, init)
    y = ssd_output(xc, bc, cc, a_cum, prev)
    return y.reshape(xdt.shape), final


def ssd_final(xdt, a, bm, init):
    return ssd_states(to_chunks(xdt), to_chunks(a), to_chunks(bm), init)[2]


def gated_rmsnorm(y, z, w):
    g = (y * jax.nn.silu(z)).astype(jnp.float32)
    g = g.reshape(*g.shape[:-1], SSM_GROUPS, D_SSM // SSM_GROUPS)
    g = g * lax.rsqrt(jnp.mean(jnp.square(g), axis=-1, keepdims=True) + LN_EPS)
    return (g.reshape(y.shape) * w).astype(y.dtype)


def split_heads(xb_or_xbc, bsz, length):
    xs = xb_or_xbc[..., :D_SSM].reshape(bsz, length, SSM_GROUPS, HEADS_PER_GROUP, SSM_HEADDIM)
    bm = xb_or_xbc[..., D_SSM:D_XB].reshape(bsz, length, SSM_GROUPS, SSM_STATE)
    return xs, bm


def token_mixer(u, p, init_f, init_b, conf_conv):
    bsz, length = u.shape[:2]
    proj = u @ p["w_in"]
    z = proj[..., :OFF_DT]
    dt_raw = proj[..., OFF_DT:OFF_X]
    xbc = ssm_conv(proj[..., OFF_X:OFF_GLU], p["conv_w"], p["conv_b"])
    glu = proj[..., OFF_GLU:]
    xs, bm = split_heads(xbc, bsz, length)
    cm = xbc[..., D_XB:].reshape(bsz, length, SSM_GROUPS, SSM_STATE)
    xdt_f, a_f = ssd_direction(xs, dt_raw[..., :SSM_HEADS], p["dt_bias_f"], p["a_log_f"])
    xdt_b, a_b = ssd_direction(xs, dt_raw[..., SSM_HEADS:], p["dt_bias_b"], p["a_log_b"])
    y_f, fin_f = ssd_scan(xdt_f, a_f, bm, cm, init_f)
    y_b, fin_b = ssd_scan(flip(xdt_b), flip(a_b), flip(bm), flip(cm), init_b)
    y = y_f + flip(y_b) + xs * p["d_skip"].reshape(SSM_GROUPS, HEADS_PER_GROUP, 1)
    y_ssm = gated_rmsnorm(y.reshape(bsz, length, D_SSM), z, p["ssm_norm_w"])
    val, gate = jnp.split(glu, 2, axis=-1)
    v = conf_conv(val * jax.nn.sigmoid(gate), p["conf_conv_w"], p["conf_conv_b"])
    v = jax.nn.silu(layer_norm(v, p["conf_ln_g"], p["conf_ln_b"]))
    out = jnp.concatenate([y_ssm, v], axis=-1) @ p["w_out"]
    return out, fin_f, fin_b


def context_final_states(u, p, init):
    bsz, length = u.shape[:2]
    cols = u @ p["w_in"][:, OFF_DT:OFF_C]
    dt_raw = cols[..., :2 * SSM_HEADS]
    xb = ssm_conv(cols[..., 2 * SSM_HEADS:], p["conv_w"][:, :D_XB], p["conv_b"][:D_XB])
    xs, bm = split_heads(xb, bsz, length)
    xdt_f, a_f = ssd_direction(xs, dt_raw[..., :SSM_HEADS], p["dt_bias_f"], p["a_log_f"])
    xdt_b, a_b = ssd_direction(xs, dt_raw[..., SSM_HEADS:], p["dt_bias_b"], p["a_log_b"])
    fin_f = ssd_final(xdt_f, a_f, bm, init)
    fin_b = ssd_final(flip(xdt_b), flip(a_b), flip(bm), init)
    return fin_f, fin_b


def swiglu(u, w1, w3, w2):
    return (jax.nn.silu(u @ w1) * (u @ w3)) @ w2


def moe_swiglu(u, router_w, router_b, w1, w3, w2):
    logits = (u @ router_w + router_b).astype(jnp.float32)
    top_val, top_idx = lax.top_k(logits, TOP_K)
    top_w = jax.nn.softmax(top_val, axis=-1)
    gates = jnp.sum(jax.nn.one_hot(top_idx, N_EXPERTS, dtype=jnp.float32) * top_w[..., None],
                    axis=-2).astype(u.dtype)
    out = jnp.zeros_like(u)
    for e in range(N_EXPERTS):
        out = out + gates[..., e:e + 1] * swiglu(u, w1[e], w3[e], w2[e])
    return out


def setup_inputs(seed: int = 0) -> dict:
    key = jax.random.key(seed)
    keys = iter(jax.random.split(key, 64))

    def normal(shape, scale):
        return jax.random.normal(next(keys), shape, jnp.float32) * scale

    def dt_bias(shape):
        dt = jnp.exp(jax.random.uniform(next(keys), shape, jnp.float32,
                                        math.log(DT_MIN), math.log(DT_MAX)))
        return dt + jnp.log(-jnp.expm1(-dt))

    def a_log(shape):
        return jnp.log(jax.random.uniform(next(keys), shape, jnp.float32, A_MIN, A_MAX))

    n_dense = (DEPTH + 1) // 2
    n_moe = DEPTH // 2
    return {
        "x": normal((BATCH, SEQ, D_MODEL), 1.0),
        "c": normal((BATCH, D_MODEL), 1.0),
        "ctx": normal((BATCH, CTX_LEN, D_MODEL), 1.0),
        "c_ctx": normal((D_MODEL,), 1.0),
        "ada_w": normal((DEPTH, D_MODEL, 6 * D_MODEL), 0.5 * D_MODEL ** -0.5),
        "ada_b": normal((DEPTH, 6 * D_MODEL), 0.02),
        "w_in": normal((DEPTH, D_MODEL, D_IN_PROJ), D_MODEL ** -0.5),
        "mamba_conv_w": normal((DEPTH, SSM_CONV, D_XBC), SSM_CONV ** -0.5),
        "mamba_conv_b": normal((DEPTH, D_XBC), 0.02),
        "dt_bias_fwd": dt_bias((DEPTH, SSM_HEADS)),
        "dt_bias_bwd": dt_bias((DEPTH, SSM_HEADS)),
        "a_log_fwd": a_log((DEPTH, SSM_HEADS)),
        "a_log_bwd": a_log((DEPTH, SSM_HEADS)),
        "d_skip": 1.0 + normal((DEPTH, SSM_HEADS), 0.1),
        "ssm_norm_w": 1.0 + normal((DEPTH, D_SSM), 0.02),
        "conf_conv_w": normal((DEPTH, CONF_WIDTH, D_CONF), CONF_WIDTH ** -0.5),
        "conf_conv_b": normal((DEPTH, D_CONF), 0.02),
        "conf_ln_g": 1.0 + normal((DEPTH, D_CONF), 0.02),
        "conf_ln_b": normal((DEPTH, D_CONF), 0.02),
        "w_out": normal((DEPTH, D_MIX, D_MODEL), BETA * D_MIX ** -0.5),
        "ln1_g": 1.0 + normal((DEPTH, D_MODEL), 0.02),
        "ln1_b": normal((DEPTH, D_MODEL), 0.02),
        "ln2_g": 1.0 + normal((DEPTH, D_MODEL), 0.02),
        "ln2_b": normal((DEPTH, D_MODEL), 0.02),
        "ffn_w1": normal((n_dense, D_MODEL, FFN_DENSE), D_MODEL ** -0.5),
        "ffn_w3": normal((n_dense, D_MODEL, FFN_DENSE), D_MODEL ** -0.5),
        "ffn_w2": normal((n_dense, FFN_DENSE, D_MODEL), BETA * FFN_DENSE ** -0.5),
        "router_w": normal((n_moe, D_MODEL, N_EXPERTS), D_MODEL ** -0.5),
        "router_b": normal((n_moe, N_EXPERTS), 0.01),
        "moe_w1": normal((n_moe, N_EXPERTS, D_MODEL, FFN_EXPERT), D_MODEL ** -0.5),
        "moe_w3": normal((n_moe, N_EXPERTS, D_MODEL, FFN_EXPERT), D_MODEL ** -0.5),
        "moe_w2": normal((n_moe, N_EXPERTS, FFN_EXPERT, D_MODEL), BETA * FFN_EXPERT ** -0.5),
    }


def reference(x, c, ctx, c_ctx, ada_w, ada_b, w_in, mamba_conv_w, mamba_conv_b,
              dt_bias_fwd, dt_bias_bwd, a_log_fwd, a_log_bwd, d_skip, ssm_norm_w,
              conf_conv_w, conf_conv_b, conf_ln_g, conf_ln_b, w_out,
              ln1_g, ln1_b, ln2_g, ln2_b, ffn_w1, ffn_w3, ffn_w2,
              router_w, router_b, moe_w1, moe_w3, moe_w2):
    bsz, seq, _ = x.shape
    rows = seq // GRID_W
    cond = jax.nn.silu(c)
    cond_ctx = jax.nn.silu(c_ctx)
    zero_state = jnp.zeros((bsz, SSM_GROUPS, HEADS_PER_GROUP, SSM_HEADDIM, SSM_STATE), x.dtype)

    def channel_mixer(u, i):
        j = i // 2
        if i % 2 == 0:
            return swiglu(u, ffn_w1[j], ffn_w3[j], ffn_w2[j])
        return moe_swiglu(u, router_w[j], router_b[j], moe_w1[j], moe_w3[j], moe_w2[j])

    h, h_ctx = x, ctx
    for i in range(DEPTH):
        p = {"w_in": w_in[i], "conv_w": mamba_conv_w[i], "conv_b": mamba_conv_b[i],
             "dt_bias_f": dt_bias_fwd[i], "dt_bias_b": dt_bias_bwd[i],
             "a_log_f": a_log_fwd[i], "a_log_b": a_log_bwd[i], "d_skip": d_skip[i],
             "ssm_norm_w": ssm_norm_w[i], "conf_conv_w": conf_conv_w[i],
             "conf_conv_b": conf_conv_b[i], "conf_ln_g": conf_ln_g[i],
             "conf_ln_b": conf_ln_b[i], "w_out": w_out[i]}
        mods = jnp.split(cond @ ada_w[i] + ada_b[i], 6, axis=-1)
        sh1, sc1, g1, sh2, sc2, g2 = [m[:, None, :] for m in mods]
        csh1, csc1, cg1, csh2, csc2, cg2 = jnp.split(cond_ctx @ ada_w[i] + ada_b[i], 6)

        u_ctx = modulate(h_ctx, csh1, csc1)
        if i == DEPTH - 1:
            fin_f, fin_b = context_final_states(u_ctx, p, zero_state)
        else:
            mix_ctx, fin_f, fin_b = token_mixer(u_ctx, p, zero_state, zero_state, dwconv1d)
            h_ctx = layer_norm(ALPHA * h_ctx + cg1 * mix_ctx, ln1_g[i], ln1_b[i])
            h_ctx = layer_norm(ALPHA * h_ctx + cg2 * channel_mixer(modulate(h_ctx, csh2, csc2), i),
                               ln2_g[i], ln2_b[i])

        grid_conv = functools.partial(dwconv_grid, rows=rows, vertical=(i % 2 == 1))
        mix, _, _ = token_mixer(modulate(h, sh1, sc1), p, fin_f, fin_b, grid_conv)
        h = layer_norm(ALPHA * h + g1 * mix, ln1_g[i], ln1_b[i])
        h = layer_norm(ALPHA * h + g2 * channel_mixer(modulate(h, sh2, sc2), i), ln2_g[i], ln2_b[i])
    return h
```

```python
import functools

import jax
import jax.numpy as jnp
from jax import lax
from jax.experimental import pallas as pl
from jax.experimental.pallas import tpu as pltpu

F32 = jnp.float32
BF16 = jnp.bfloat16

D_MODEL = 4096
DEPTH = 2
GRID_W = 64
D_SSM = 2048
D_CONF = 2048
SSM_HEADDIM = 64
SSM_HEADS = 32
SSM_GROUPS = 8
HEADS_PER_GROUP = 4
SSM_STATE = 128
SSM_CONV = 5
CHUNK = 128
GROUP_W = HEADS_PER_GROUP * SSM_HEADDIM
CONF_WIDTH = 31
CONF_HALF = CONF_WIDTH // 2
OFF_DT = D_SSM
OFF_X = OFF_DT + 2 * SSM_HEADS
N_MAIN = 10240
COL_X, COL_BC, COL_VAL, COL_GATE = 1, 2, 3, 4
FFN_DENSE = 11008
FFN_PAD = 11264
N_EXPERTS = 8
FFN_EXPERT = 3584
LN_EPS = 1e-5
ALPHA = (2 * DEPTH) ** 0.25

LANES = 128
SUBLANES = 8
ROW_TILE = 256
MOE_TM = 512
VMEM_LIMIT = 56 * 1024 * 1024


def _cparams(sem, vmem=VMEM_LIMIT):
    return pltpu.CompilerParams(dimension_semantics=sem, vmem_limit_bytes=vmem)


def _sigmoid(x):
    return 1.0 / (1.0 + jnp.exp(-x))


def _silu(x):
    return x * _sigmoid(x)


def _softplus(x):
    return jnp.maximum(x, 0.0) + jnp.log(1.0 + jnp.exp(-jnp.abs(x)))


def _standardize(x):
    mu = jnp.mean(x, axis=-1, keepdims=True)
    xc = x - mu
    var = jnp.mean(xc * xc, axis=-1, keepdims=True)
    return xc * lax.rsqrt(var + LN_EPS)


def _pick_tile(n, target, mult=16):
    best = None
    for t in range(mult, min(n, target) + 1, mult):
        if n % t == 0:
            best = t
    assert best is not None, (n, target)
    return best


def _ada_kernel(c_ref, w_ref, b_ref, o_ref):
    cond = _silu(c_ref[...]).astype(BF16)
    o_ref[...] = jnp.dot(cond, w_ref[...].astype(BF16), preferred_element_type=F32) + b_ref[...]


def _ada_mods(cond_rows, ada_w, ada_b):
    depth, d, n = ada_w.shape
    rows = cond_rows.shape[0]
    tn = 512
    return pl.pallas_call(
        _ada_kernel,
        grid=(depth, n // tn),
        in_specs=[
            pl.BlockSpec((rows, d), lambda l, j: (0, 0)),
            pl.BlockSpec((None, d, tn), lambda l, j: (l, 0, j)),
            pl.BlockSpec((None, 1, tn), lambda l, j: (l, 0, j)),
        ],
        out_specs=pl.BlockSpec((None, rows, tn), lambda l, j: (l, 0, j)),
        out_shape=jax.ShapeDtypeStruct((depth, rows, n), F32),
        compiler_params=_cparams(("arbitrary", "arbitrary")),
        name="ada_mods",
    )(cond_rows, ada_w, ada_b.reshape(depth, 1, n))


class _Geom:
    def __init__(self, bsz, seq, ctx_len):
        self.bsz, self.seq, self.ctx = bsz, seq, ctx_len
        self.n_lat = bsz * seq
        self.n_all = self.n_lat + bsz * ctx_len
        assert seq % ROW_TILE == 0 and ctx_len % ROW_TILE == 0 and seq % GRID_W == 0
        self.lat_tiles = self.n_lat // ROW_TILE
        self.tiles_per_seq = seq // ROW_TILE
        self.tiles_per_ctx = ctx_len // ROW_TILE

    def mod_row(self, i):
        return jnp.where(i < self.lat_tiles, i // self.tiles_per_seq, self.bsz)

    def seq_edges(self, i):
        lat = i < self.lat_tiles
        j = i - self.lat_tiles
        first = jnp.where(lat, i % self.tiles_per_seq == 0, j % self.tiles_per_ctx == 0)
        last = jnp.where(lat, i % self.tiles_per_seq == self.tiles_per_seq - 1,
                         j % self.tiles_per_ctx == self.tiles_per_ctx - 1)
        return first, last


def _mod_spec(geom, slot):
    return pl.BlockSpec((None, 1, D_MODEL), lambda i: (geom.mod_row(i) * 6 + slot, 0, 0))


def _pack_mod_kernel(x_ref, c_ref, sh_ref, sc_ref, h_ref, u_ref, *, lat_tiles):
    i = pl.program_id(0)

    def emit(v):
        h_ref[...] = v
        u_ref[...] = (_standardize(v) * (1.0 + sc_ref[...]) + sh_ref[...]).astype(BF16)

    @pl.when(i < lat_tiles)
    def _():
        emit(x_ref[...])

    @pl.when(i >= lat_tiles)
    def _():
        emit(c_ref[...])


def _pack_modulate(geom, x2, ctx2, mods):
    lt = geom.lat_tiles
    n_tiles = geom.n_all // ROW_TILE
    row = pl.BlockSpec((ROW_TILE, D_MODEL), lambda i: (i, 0))
    return pl.pallas_call(
        functools.partial(_pack_mod_kernel, lat_tiles=lt),
        grid=(n_tiles,),
        in_specs=[
            pl.BlockSpec((ROW_TILE, D_MODEL), lambda i: (jnp.minimum(i, lt - 1), 0)),
            pl.BlockSpec((ROW_TILE, D_MODEL), lambda i: (jnp.maximum(i - lt, 0), 0)),
            _mod_spec(geom, 0), _mod_spec(geom, 1),
        ],
        out_specs=[row, row],
        out_shape=[jax.ShapeDtypeStruct((geom.n_all, D_MODEL), F32),
                   jax.ShapeDtypeStruct((geom.n_all, D_MODEL), BF16)],
        compiler_params=_cparams(("arbitrary",)),
        name="pack_modulate",
    )(x2, ctx2, mods, mods)


def _mm_kernel(*refs, n_pairs):
    o_ref = refs[-1]
    acc = jnp.dot(refs[0][...], refs[1][...], preferred_element_type=F32)
    for p in range(1, n_pairs):
        acc = acc + jnp.dot(refs[2 * p][...], refs[2 * p + 1][...], preferred_element_type=F32)
    o_ref[...] = acc.astype(o_ref.dtype)


def _matmul(pairs, rows, tm, tn, out_dtype, name):
    n = pairs[0][1].shape[1]
    assert rows % tm == 0 and n % tn == 0
    in_specs, args = [], []
    for a, w in pairs:
        k = a.shape[1]
        in_specs += [pl.BlockSpec((tm, k), lambda i, j: (i, 0)), pl.BlockSpec((k, tn), lambda i, j: (0, j))]
        args += [a, w]
    return pl.pallas_call(
        functools.partial(_mm_kernel, n_pairs=len(pairs)),
        grid=(rows // tm, n // tn),
        in_specs=in_specs,
        out_specs=pl.BlockSpec((tm, tn), lambda i, j: (i, j)),
        out_shape=jax.ShapeDtypeStruct((rows, n), out_dtype),
        compiler_params=_cparams(("arbitrary", "arbitrary")),
        name=name,
    )(*args)


def _gated_up_kernel(a_ref, w1_ref, w3_ref, o_ref):
    a = a_ref[...]
    h1 = jnp.dot(a, w1_ref[...], preferred_element_type=F32)
    h3 = jnp.dot(a, w3_ref[...], preferred_element_type=F32)
    o_ref[...] = (_silu(h1) * h3).astype(o_ref.dtype)


def _gated_up(a, w1, w3, rows, tm, tn):
    k, n = w1.shape
    return pl.pallas_call(
        _gated_up_kernel,
        grid=(rows // tm, n // tn),
        in_specs=[pl.BlockSpec((tm, k), lambda i, j: (i, 0)),
                  pl.BlockSpec((k, tn), lambda i, j: (0, j)),
                  pl.BlockSpec((k, tn), lambda i, j: (0, j))],
        out_specs=pl.BlockSpec((tm, tn), lambda i, j: (i, j)),
        out_shape=jax.ShapeDtypeStruct((rows, n), BF16),
        compiler_params=_cparams(("arbitrary", "arbitrary")),
        name="ffn_up",
    )(a, w1, w3)


def _mm_acc_kernel(a_ref, w_ref, o_ref, acc_ref):
    k = pl.program_id(2)

    @pl.when(k == 0)
    def _():
        acc_ref[...] = jnp.zeros_like(acc_ref)

    acc_ref[...] += jnp.dot(a_ref[...], w_ref[...], preferred_element_type=F32)

    @pl.when(k == pl.num_programs(2) - 1)
    def _():
        o_ref[...] = acc_ref[...]


def _matmul_ktiled(a, w, rows, tm, tn, tk, name):
    kk, n = w.shape
    assert rows % tm == 0 and n % tn == 0 and kk % tk == 0
    return pl.pallas_call(
        _mm_acc_kernel,
        grid=(rows // tm, n // tn, kk // tk),
        in_specs=[pl.BlockSpec((tm, tk), lambda i, j, k: (i, k)),
                  pl.BlockSpec((tk, tn), lambda i, j, k: (k, j))],
        out_specs=pl.BlockSpec((tm, tn), lambda i, j, k: (i, j)),
        out_shape=jax.ShapeDtypeStruct((rows, n), F32),
        scratch_shapes=[pltpu.VMEM((tm, tn), F32)],
        compiler_params=_cparams(("arbitrary", "arbitrary", "arbitrary")),
        name=name,
    )(a, w)


def _conv5_kernel(xm_ref, xp_ref, xn_ref, bm_ref, bp_ref, bn_ref, w_ref, b_ref, xo_ref, bco_ref, scr_ref,
                  *, geom):
    i = pl.program_id(0)
    first, last = geom.seq_edges(i)
    half = D_SSM
    pad = SUBLANES
    for part, (m_ref, p_ref, n_ref) in enumerate(((xm_ref, xp_ref, xn_ref), (bm_ref, bp_ref, bn_ref))):
        lo = part * half
        scr_ref[0:pad, lo:lo + half] = jnp.where(first, 0.0, p_ref[...])
        scr_ref[pad:pad + ROW_TILE, lo:lo + half] = m_ref[...]
        scr_ref[pad + ROW_TILE:2 * pad + ROW_TILE, lo:lo + half] = jnp.where(last, 0.0, n_ref[...])
    rows_blk, lane_blk = 64, 512
    base = pad - SSM_CONV // 2
    for c0 in range(0, 2 * half, lane_blk):
        wc = w_ref[:, c0:c0 + lane_blk]
        bias = b_ref[:, c0:c0 + lane_blk]
        for r0 in range(0, ROW_TILE, rows_blk):
            acc = bias
            for k in range(SSM_CONV):
                acc = acc + scr_ref[base + r0 + k:base + r0 + k + rows_blk, c0:c0 + lane_blk] * wc[k:k + 1, :]
            val = _silu(acc)
            if c0 < half:
                xo_ref[r0:r0 + rows_blk, c0:c0 + lane_blk] = val
            else:
                bco_ref[r0:r0 + rows_blk, c0 - half:c0 - half + lane_blk] = val.astype(BF16)


def _ssm_conv(geom, proj, conv_w, conv_b):
    n_tiles = geom.n_all // ROW_TILE
    sub = ROW_TILE // SUBLANES
    last_blk = geom.n_all // SUBLANES - 1

    def main(col):
        return pl.BlockSpec((ROW_TILE, D_SSM), lambda i: (i, col))

    def prev(col):
        return pl.BlockSpec((SUBLANES, D_SSM), lambda i: (jnp.maximum(i * sub - 1, 0), col))

    def nxt(col):
        return pl.BlockSpec((SUBLANES, D_SSM), lambda i: (jnp.minimum((i + 1) * sub, last_blk), col))

    width = 2 * D_SSM
    return pl.pallas_call(
        functools.partial(_conv5_kernel, geom=geom),
        grid=(n_tiles,),
        in_specs=[main(COL_X), prev(COL_X), nxt(COL_X), main(COL_BC), prev(COL_BC), nxt(COL_BC),
                  pl.BlockSpec((SSM_CONV, width), lambda i: (0, 0)),
                  pl.BlockSpec((1, width), lambda i: (0, 0))],
        out_specs=[pl.BlockSpec((ROW_TILE, D_SSM), lambda i: (i, 0)),
                   pl.BlockSpec((ROW_TILE, D_SSM), lambda i: (i, 0))],
        out_shape=[jax.ShapeDtypeStruct((geom.n_all, D_SSM), F32),
                   jax.ShapeDtypeStruct((geom.n_all, D_SSM), BF16)],
        scratch_shapes=[pltpu.VMEM((ROW_TILE + 2 * SUBLANES, width), F32)],
        compiler_params=_cparams(("arbitrary",)),
        name="ssm_conv",
    )(proj, proj, proj, proj, proj, proj, conv_w, conv_b.reshape(1, width))


def _split3_dot(lhs_fn, v):
    v1 = v.astype(BF16)
    r1 = v - v1.astype(F32)
    v2 = r1.astype(BF16)
    v3 = (r1 - v2.astype(F32)).astype(BF16)
    return lhs_fn(v1) + lhs_fn(v2) + lhs_fn(v3)


def _ssd_kernel(*refs, rev, final):
    if final:
        (x_ref, bc_ref, dtr_ref, dtt_ref, pbr_ref, pbc_ref, alr_ref, alc_ref,
         yb_ref, z_ref, dsk_ref, nw_ref, o_ref, s_ref) = refs
    else:
        x_ref, bc_ref, dtr_ref, dtt_ref, pbr_ref, pbc_ref, alr_ref, alc_ref, o_ref, s_ref = refs
    step = pl.program_id(1)

    @pl.when(step == 0)
    def _():
        s_ref[...] = jnp.zeros_like(s_ref)

    n = CHUNK
    row = lax.broadcasted_iota(jnp.int32, (n, n), 0)
    col = lax.broadcasted_iota(jnp.int32, (n, n), 1)
    tri = (row <= col) if rev else (row >= col)
    cum_l = tri.astype(BF16)
    cum_r = jnp.logical_not(tri) | (row == col)
    cum_r = cum_r.astype(BF16)
    h_off = SSM_HEADS if rev else 0
    edge = 0 if rev else n - 1

    dt = _softplus(dtr_ref[...] + pbr_ref[...])
    a = -dt * jnp.exp(alr_ref[...])
    dtt = _softplus(dtt_ref[...] + pbc_ref[...])
    at = -dtt * jnp.exp(alc_ref[...])
    acum = _split3_dot(lambda v: jnp.dot(cum_l, v, preferred_element_type=F32), a)
    acum_t = _split3_dot(lambda v: jnp.dot(v, cum_r, preferred_element_type=F32), at)

    lane_lo = lax.broadcasted_iota(jnp.int32, (n, LANES), 1) < SSM_HEADDIM
    lane_w = lax.broadcasted_iota(jnp.int32, (n, GROUP_W), 1)

    for g in range(SSM_GROUPS):
        b_g = bc_ref[:, g * SSM_STATE:(g + 1) * SSM_STATE]
        c_g = bc_ref[:, SSM_GROUPS * SSM_STATE + g * SSM_STATE:SSM_GROUPS * SSM_STATE + (g + 1) * SSM_STATE]
        x_g = x_ref[:, g * GROUP_W:(g + 1) * GROUP_W]
        x_gb = x_g.astype(BF16)
        s_g = s_ref[g]
        cb = lax.dot_general(c_g, b_g, (((1,), (1,)), ((), ())), preferred_element_type=F32)

        m_parts, ecols, wcols, etots = [], [], [], []
        for e in range(HEADS_PER_GROUP):
            h = g * HEADS_PER_GROUP + e
            colb = jnp.broadcast_to(acum[:, h_off + h:h_off + h + 1], (n, n))
            rowb = jnp.broadcast_to(acum_t[h:h + 1, :], (n, n))
            dtrow = jnp.broadcast_to(dtt[h:h + 1, :], (n, n))
            dtcol = jnp.broadcast_to(dt[:, h_off + h:h_off + h + 1], (n, n))
            decay = jnp.exp(jnp.where(tri, colb - rowb, -jnp.inf))
            m_parts.append((cb * decay * dtrow).astype(BF16))
            ecol = jnp.exp(colb)
            tot = jnp.broadcast_to(colb[edge:edge + 1, :], (n, n))
            ecols.append(ecol)
            wcols.append(jnp.exp(tot - colb) * dtcol)
            etots.append(ecol[edge:edge + 1, :])

        def per_head(parts):
            return jnp.concatenate([jnp.where(lane_lo[:parts[0].shape[0]], parts[0], parts[1]),
                                    jnp.where(lane_lo[:parts[0].shape[0]], parts[2], parts[3])], axis=1)

        scale_e = per_head(ecols)
        scale_w = per_head(wcols)
        scale_t = per_head(etots)

        y = None
        for pair in range(HEADS_PER_GROUP // 2):
            e0, e1 = 2 * pair, 2 * pair + 1
            lhs = jnp.concatenate([m_parts[e0], m_parts[e1]], axis=1)
            r0 = jnp.where((lane_w // SSM_HEADDIM) == e0, x_gb, jnp.zeros_like(x_gb))
            r1 = jnp.where((lane_w // SSM_HEADDIM) == e1, x_gb, jnp.zeros_like(x_gb))
            part = jnp.dot(lhs, jnp.concatenate([r0, r1], axis=0), preferred_element_type=F32)
            y = part if y is None else y + part
        y = y + jnp.dot(c_g, s_g.astype(BF16), preferred_element_type=F32) * scale_e
        xw = (x_g * scale_w).astype(BF16)
        b_t = jnp.transpose(b_g.astype(F32)).astype(BF16)
        s_ref[g] = s_g * scale_t + jnp.dot(b_t, xw, preferred_element_type=F32)

        cols = slice(g * GROUP_W, (g + 1) * GROUP_W)
        if final:
            y = y + yb_ref[:, cols] + x_g * dsk_ref[:, cols]
            gz = y * _silu(z_ref[:, cols])
            ms = jnp.mean(gz * gz, axis=-1, keepdims=True)
            o_ref[:, cols] = (gz * lax.rsqrt(ms + LN_EPS) * nw_ref[:, cols]).astype(o_ref.dtype)
        else:
            o_ref[:, cols] = y


def _ssd_pass(geom, x, bc, dtr, dtt, dt_bias, a_log, rev, final_args=None):
    ncl = geom.seq // CHUNK
    ncc = geom.ctx // CHUNK
    lat_chunks = geom.n_lat // CHUNK

    def chunk(b, s):
        if rev:
            return jnp.where(s < ncc, lat_chunks + b * ncc + (ncc - 1 - s), b * ncl + (ncl - 1 - (s - ncc)))
        return jnp.where(s < ncc, lat_chunks + b * ncc + s, b * ncl + (s - ncc))

    def rows(width):
        return pl.BlockSpec((CHUNK, width), lambda b, s: (chunk(b, s), 0))

    d_idx = 1 if rev else 0
    small = lambda shape: pl.BlockSpec(shape, lambda b, s: (0, 0))
    in_specs = [rows(D_SSM), rows(D_SSM), rows(LANES),
                pl.BlockSpec((SSM_HEADS, CHUNK), lambda b, s: (d_idx, chunk(b, s))),
                small((1, LANES)), small((SSM_HEADS, 1)), small((1, LANES)), small((SSM_HEADS, 1))]
    lane_pad = (d_idx * SSM_HEADS, LANES - (d_idx + 1) * SSM_HEADS)
    args = [x, bc, dtr, dtt, jnp.pad(dt_bias, lane_pad).reshape(1, LANES), dt_bias.reshape(SSM_HEADS, 1),
            jnp.pad(a_log, lane_pad).reshape(1, LANES), a_log.reshape(SSM_HEADS, 1)]
    final = final_args is not None
    if final:
        y_other, proj, dsk, nw = final_args
        in_specs += [rows(D_SSM), rows(D_SSM), small((1, D_SSM)), small((1, D_SSM))]
        args += [y_other, proj, dsk, nw]
    return pl.pallas_call(
        functools.partial(_ssd_kernel, rev=rev, final=final),
        grid=(geom.bsz, ncc + ncl),
        in_specs=in_specs,
        out_specs=rows(D_SSM),
        out_shape=jax.ShapeDtypeStruct((geom.n_all, D_SSM), BF16 if final else F32),
        scratch_shapes=[pltpu.VMEM((SSM_GROUPS, SSM_STATE, GROUP_W), F32)],
        compiler_params=_cparams(("arbitrary", "arbitrary")),
        name="ssd_fwd" if final else "ssd_bwd",
    )(*args)


CONF_LANE_BLK = 256
CONF_NBLK = D_CONF // CONF_LANE_BLK
CONF_GAP = 16
CONF_STRIDE = GRID_W + CONF_GAP


def _ln_swish_rows(res_ref, g_ref, b_ref, o_ref, rows_blk=64):
    n_rows = o_ref.shape[0]
    for r0 in range(0, n_rows, rows_blk):
        v = jnp.concatenate([res_ref[c, r0:r0 + rows_blk, :] for c in range(CONF_NBLK)], axis=1)
        v = _standardize(v) * g_ref[...] + b_ref[...]
        o_ref[r0:r0 + rows_blk, :] = _silu(v).astype(o_ref.dtype)


def _conf_rows_kernel(val_ref, gate_ref, w_ref, cb_ref, g_ref, b_ref, o_ref, scr_ref, res_ref, *, lat_tiles):
    i = pl.program_id(0)

    def run_conv(run_len, n_runs):
        stride = run_len + CONF_GAP
        gap = jnp.zeros((CONF_GAP, CONF_LANE_BLK), F32)
        for c in range(CONF_NBLK):
            lanes = slice(c * CONF_LANE_BLK, (c + 1) * CONF_LANE_BLK)
            scr_ref[c, 0:CONF_GAP, :] = gap
            for r in range(n_runs):
                lo = CONF_GAP + r * stride
                for q in range(0, run_len, GRID_W):
                    src = slice(r * run_len + q, r * run_len + q + GRID_W)
                    scr_ref[c, lo + q:lo + q + GRID_W, :] = val_ref[src, lanes] * _sigmoid(gate_ref[src, lanes])
                scr_ref[c, lo + run_len:lo + run_len + CONF_GAP, :] = gap

        def body(c, carry):
            w = w_ref[c]
            for r in range(n_runs):
                for q in range(0, run_len, GRID_W):
                    acc = jnp.broadcast_to(cb_ref[c], (GRID_W, CONF_LANE_BLK))
                    off = CONF_GAP - CONF_HALF + r * stride + q
                    for k in range(CONF_WIDTH):
                        acc = acc + scr_ref[c, off + k:off + k + GRID_W, :] * w[k:k + 1, :]
                    res_ref[c, r * run_len + q:r * run_len + q + GRID_W, :] = acc
            return carry

        lax.fori_loop(0, CONF_NBLK, body, 0)

    @pl.when(i < lat_tiles)
    def _():
        run_conv(GRID_W, ROW_TILE // GRID_W)

    @pl.when(i >= lat_tiles)
    def _():
        run_conv(ROW_TILE, 1)

    _ln_swish_rows(res_ref, g_ref, b_ref, o_ref)


def _conformer_rows(geom, proj, w, cb, ln_g, ln_b):
    assert geom.ctx == ROW_TILE
    n_tiles = geom.n_all // ROW_TILE
    w3 = jnp.transpose(w.reshape(CONF_WIDTH, CONF_NBLK, CONF_LANE_BLK), (1, 0, 2))
    cb3 = cb.reshape(CONF_NBLK, 1, CONF_LANE_BLK)
    scr_rows = CONF_GAP + max((ROW_TILE // GRID_W) * CONF_STRIDE, ROW_TILE + CONF_GAP)
    full = lambda shape: pl.BlockSpec(shape, lambda i: (0,) * len(shape))
    return pl.pallas_call(
        functools.partial(_conf_rows_kernel, lat_tiles=geom.lat_tiles),
        grid=(n_tiles,),
        in_specs=[pl.BlockSpec((ROW_TILE, D_CONF), lambda i: (i, COL_VAL)),
                  pl.BlockSpec((ROW_TILE, D_CONF), lambda i: (i, COL_GATE)),
                  full((CONF_NBLK, CONF_WIDTH, CONF_LANE_BLK)), full((CONF_NBLK, 1, CONF_LANE_BLK)),
                  full((1, D_CONF)), full((1, D_CONF))],
        out_specs=pl.BlockSpec((ROW_TILE, D_CONF), lambda i: (i, 0)),
        out_shape=jax.ShapeDtypeStruct((geom.n_all, D_CONF), BF16),
        scratch_shapes=[pltpu.VMEM((CONF_NBLK, scr_rows, CONF_LANE_BLK), F32),
                        pltpu.VMEM((CONF_NBLK, ROW_TILE, CONF_LANE_BLK), F32)],
        compiler_params=_cparams(("arbitrary",)),
        name="conformer_rows",
    )(proj, proj, w3, cb3, ln_g.reshape(1, D_CONF), ln_b.reshape(1, D_CONF))


def _conf_cols_kernel(val_ref, gate_ref, w_ref, cb_ref, o_ref, scr_ref, *, seq):
    halo = CONF_HALF * GRID_W
    scr_ref[0:halo, :] = jnp.zeros((halo, LANES), F32)
    scr_ref[halo + seq:2 * halo + seq, :] = jnp.zeros((halo, LANES), F32)
    blk = 256
    for r0 in range(0, seq, blk):
        scr_ref[halo + r0:halo + r0 + blk, :] = val_ref[r0:r0 + blk, :] * _sigmoid(gate_ref[r0:r0 + blk, :])
    rows_blk = 128

    def body(rb, carry):
        acc = jnp.broadcast_to(cb_ref[...], (rows_blk, LANES))
        for k in range(CONF_WIDTH):
            start = pl.multiple_of(rb * rows_blk + k * GRID_W, GRID_W)
            acc = acc + scr_ref[pl.ds(start, rows_blk), :] * w_ref[k:k + 1, :]
        o_ref[pl.ds(pl.multiple_of(rb * rows_blk, rows_blk), rows_blk), :] = acc
        return carry

    lax.fori_loop(0, seq // rows_blk, body, 0)


def _conformer_cols_conv(geom, proj, w, cb):
    seq = geom.seq
    nblk = D_CONF // LANES
    val0 = COL_VAL * D_CONF // LANES
    gate0 = COL_GATE * D_CONF // LANES
    return pl.pallas_call(
        functools.partial(_conf_cols_kernel, seq=seq),
        grid=(geom.bsz, nblk),
        in_specs=[pl.BlockSpec((seq, LANES), lambda b, c: (b, val0 + c)),
                  pl.BlockSpec((seq, LANES), lambda b, c: (b, gate0 + c)),
                  pl.BlockSpec((CONF_WIDTH, LANES), lambda b, c: (0, c)),
                  pl.BlockSpec((1, LANES), lambda b, c: (0, c))],
        out_specs=pl.BlockSpec((seq, LANES), lambda b, c: (b, c)),
        out_shape=jax.ShapeDtypeStruct((geom.n_lat, D_CONF), F32),
        scratch_shapes=[pltpu.VMEM((seq + 2 * CONF_HALF * GRID_W, LANES), F32)],
        compiler_params=_cparams(("arbitrary", "arbitrary")),
        name="conformer_cols",
    )(proj, proj, w, cb.reshape(1, D_CONF))


def _ln_swish_kernel(v_ref, g_ref, b_ref, o_ref):
    o_ref[...] = _silu(_standardize(v_ref[...]) * g_ref[...] + b_ref[...]).astype(o_ref.dtype)


def _ln_swish(conv, ln_g, ln_b):
    rows = conv.shape[0]
    full = pl.BlockSpec((1, D_CONF), lambda i: (0, 0))
    return pl.pallas_call(
        _ln_swish_kernel,
        grid=(rows // ROW_TILE,),
        in_specs=[pl.BlockSpec((ROW_TILE, D_CONF), lambda i: (i, 0)), full, full],
        out_specs=pl.BlockSpec((ROW_TILE, D_CONF), lambda i: (i, 0)),
        out_shape=jax.ShapeDtypeStruct((rows, D_CONF), BF16),
        compiler_params=_cparams(("arbitrary",)),
        name="conformer_ln",
    )(conv, ln_g.reshape(1, D_CONF), ln_b.reshape(1, D_CONF))


def _top2(logits):
    lane_i = lax.broadcasted_iota(jnp.int32, logits.shape, 1)
    lane = lane_i.astype(F32)
    m1 = jnp.max(logits, axis=-1, keepdims=True)
    i1 = jnp.min(jnp.where(logits == m1, lane, float(LANES)), axis=-1, keepdims=True)
    rest = jnp.where(lane == i1, -jnp.inf, logits)
    m2 = jnp.max(rest, axis=-1, keepdims=True)
    i2 = jnp.min(jnp.where(rest == m2, lane, float(LANES)), axis=-1, keepdims=True)
    e2 = jnp.exp(m2 - m1)
    w1 = 1.0 / (1.0 + e2)
    w2 = e2 / (1.0 + e2)
    idx = jnp.where(lane_i == 0, i1, jnp.where(lane_i == 1, i2, 0.0)).astype(jnp.int32)
    wts = jnp.where(lane_i == 0, w1, jnp.where(lane_i == 1, w2, 0.0))
    return idx, wts


def _residual_norm(h, f, gate, ln_g, ln_b):
    return _standardize(ALPHA * h + gate * f) * ln_g + ln_b


def _epilogue_kernel(*refs, modulate, router):
    h_ref, f_ref, gate_ref, lg_ref, lb_ref = refs[:5]
    pos = 5
    if modulate:
        sh_ref, sc_ref = refs[pos:pos + 2]
        pos += 2
    if router:
        rwh_ref, rwl_ref, rb_ref = refs[pos:pos + 3]
        pos += 3
    outs = refs[pos:]
    hn = _residual_norm(h_ref[...], f_ref[...], gate_ref[...], lg_ref[...], lb_ref[...])
    outs[0][...] = hn
    if modulate:
        u = _standardize(hn) * (1.0 + sc_ref[...]) + sh_ref[...]
        outs[1][...] = u.astype(outs[1].dtype)
    if router:
        u_hi = u.astype(BF16)
        u_lo = (u - u_hi.astype(F32)).astype(BF16)
        logits = (jnp.dot(u_hi, rwh_ref[...], preferred_element_type=F32)
                  + jnp.dot(u_lo, rwh_ref[...], preferred_element_type=F32)
                  + jnp.dot(u_hi, rwl_ref[...], preferred_element_type=F32)) + rb_ref[...]
        lane = lax.broadcasted_iota(jnp.int32, logits.shape, 1)
        logits = jnp.where(lane < N_EXPERTS, logits, -jnp.inf)
        idx, wts = _top2(logits)
        outs[2][...] = idx
        outs[3][...] = wts


def _epilogue(geom, h, f, rows, mods, gate_slot, ln_g, ln_b, next_mods=None, next_slots=None, router=None):
    n_tiles = rows // ROW_TILE
    row = pl.BlockSpec((ROW_TILE, D_MODEL), lambda i: (i, 0))
    vec = pl.BlockSpec((1, D_MODEL), lambda i: (0, 0))
    in_specs = [row, row, _mod_spec(geom, gate_slot), vec, vec]
    args = [h, f, mods, ln_g.reshape(1, D_MODEL), ln_b.reshape(1, D_MODEL)]
    out_specs = [row]
    out_shape = [jax.ShapeDtypeStruct((rows, D_MODEL), F32)]
    modulate = next_mods is not None
    if modulate:
        in_specs += [_mod_spec(geom, next_slots[0]), _mod_spec(geom, next_slots[1])]
        args += [next_mods, next_mods]
        out_specs.append(row)
        out_shape.append(jax.ShapeDtypeStruct((rows, D_MODEL), F32 if router is not None else BF16))
    if router is not None:
        rw_hi, rw_lo, rb = router
        small = pl.BlockSpec((D_MODEL, LANES), lambda i: (0, 0))
        in_specs += [small, small, pl.BlockSpec((1, LANES), lambda i: (0, 0))]
        args += [rw_hi, rw_lo, rb]
        lane_blk = pl.BlockSpec((ROW_TILE, LANES), lambda i: (i, 0))
        out_specs += [lane_blk, lane_blk]
        out_shape += [jax.ShapeDtypeStruct((rows, LANES), jnp.int32),
                      jax.ShapeDtypeStruct((rows, LANES), F32)]
    return pl.pallas_call(
        functools.partial(_epilogue_kernel, modulate=modulate, router=router is not None),
        grid=(n_tiles,),
        in_specs=in_specs,
        out_specs=out_specs,
        out_shape=out_shape,
        compiler_params=_cparams(("arbitrary",)),
        name="epilogue",
    )(*args)


GATHER_ROWS = 256


def _row_copy(src_hbm, dst_vmem, sem, src_row, dst_row):
    return pltpu.make_async_copy(src_hbm.at[pl.ds(src_row, 1), :], dst_vmem.at[pl.ds(dst_row, 1), :], sem)


def _gather_kernel(idx_ref, src_ref, o_ref, buf_ref, sem):
    def issue(r, carry):
        _row_copy(src_ref, buf_ref, sem, idx_ref[0, 0, r], r).start()
        return carry

    lax.fori_loop(0, GATHER_ROWS, issue, 0)

    def drain(r, carry):
        _row_copy(src_ref, buf_ref, sem, 0, r).wait()
        return carry

    lax.fori_loop(0, GATHER_ROWS, drain, 0)
    o_ref[...] = buf_ref[...].astype(o_ref.dtype)


def _gather_rows(src, slot_token):
    n_slots = slot_token.shape[0]
    steps = n_slots // GATHER_ROWS
    return pl.pallas_call(
        _gather_kernel,
        grid=(steps,),
        in_specs=[pl.BlockSpec((1, 1, GATHER_ROWS), lambda i: (i, 0, 0), memory_space=pltpu.SMEM),
                  pl.BlockSpec(memory_space=pl.ANY)],
        out_specs=pl.BlockSpec((GATHER_ROWS, D_MODEL), lambda i: (i, 0)),
        out_shape=jax.ShapeDtypeStruct((n_slots, D_MODEL), BF16),
        scratch_shapes=[pltpu.VMEM((GATHER_ROWS, D_MODEL), F32), pltpu.SemaphoreType.DMA(())],
        compiler_params=_cparams(("arbitrary",)),
        name="moe_gather",
    )(slot_token.reshape(steps, 1, GATHER_ROWS), src)


def _tile_idx(w, nused_ref):
    return jnp.minimum(w, nused_ref[0] - 1)


def _moe_up_kernel(te_ref, nused_ref, a_ref, w1_ref, w3_ref, o_ref):
    used = pl.program_id(1) < nused_ref[0]

    @pl.when(used)
    def _():
        a = a_ref[...]
        h1 = jnp.dot(a, w1_ref[...], preferred_element_type=F32)
        h3 = jnp.dot(a, w3_ref[...], preferred_element_type=F32)
        o_ref[...] = (_silu(h1) * h3).astype(o_ref.dtype)

    @pl.when(jnp.logical_not(used))
    def _():
        o_ref[...] = jnp.zeros_like(o_ref)


def _moe_up(a_sorted, w1, w3, tile_expert, n_used, tf=512):
    n_slots, k = a_sorted.shape
    n_tiles = n_slots // MOE_TM
    f = w1.shape[2]
    grid_spec = pltpu.PrefetchScalarGridSpec(
        num_scalar_prefetch=2,
        grid=(f // tf, n_tiles),
        in_specs=[pl.BlockSpec((MOE_TM, k), lambda j, w, te, nu: (_tile_idx(w, nu), 0)),
                  pl.BlockSpec((None, k, tf), lambda j, w, te, nu: (te[_tile_idx(w, nu)], 0, j)),
                  pl.BlockSpec((None, k, tf), lambda j, w, te, nu: (te[_tile_idx(w, nu)], 0, j))],
        out_specs=pl.BlockSpec((MOE_TM, tf), lambda j, w, te, nu: (w, j)),
    )
    return pl.pallas_call(
        _moe_up_kernel,
        grid_spec=grid_spec,
        out_shape=jax.ShapeDtypeStruct((n_slots, f), BF16),
        compiler_params=_cparams(("arbitrary", "arbitrary")),
        name="moe_up",
    )(tile_expert, n_used, a_sorted, w1, w3)


def _moe_down_kernel(te_ref, nused_ref, a_ref, w_ref, o_ref):
    used = pl.program_id(1) < nused_ref[0]

    @pl.when(used)
    def _():
        o_ref[...] = jnp.dot(a_ref[...], w_ref[...], preferred_element_type=F32)

    @pl.when(jnp.logical_not(used))
    def _():
        o_ref[...] = jnp.zeros_like(o_ref)


def _moe_down(hid, w2, tile_expert, n_used, tn=1024):
    n_slots, k = hid.shape
    n_tiles = n_slots // MOE_TM
    n = w2.shape[2]
    grid_spec = pltpu.PrefetchScalarGridSpec(
        num_scalar_prefetch=2,
        grid=(n // tn, n_tiles),
        in_specs=[pl.BlockSpec((MOE_TM, k), lambda j, w, te, nu: (_tile_idx(w, nu), 0)),
                  pl.BlockSpec((None, k, tn), lambda j, w, te, nu: (te[_tile_idx(w, nu)], 0, j))],
        out_specs=pl.BlockSpec((MOE_TM, tn), lambda j, w, te, nu: (w, j)),
    )
    return pl.pallas_call(
        _moe_down_kernel,
        grid_spec=grid_spec,
        out_shape=jax.ShapeDtypeStruct((n_slots, n), F32),
        compiler_params=_cparams(("arbitrary", "arbitrary")),
        name="moe_down",
    )(tile_expert, n_used, hid, w2)


COMBINE_ROWS = 128


def _combine_kernel(pos_ref, h_ref, wts_ref, gate_ref, lg_ref, lb_ref, y_ref, o_ref, buf_ref, sem):
    def issue(r, carry):
        _row_copy(y_ref, buf_ref.at[0], sem, pos_ref[0, 0, 2 * r], r).start()
        _row_copy(y_ref, buf_ref.at[1], sem, pos_ref[0, 0, 2 * r + 1], r).start()
        return carry

    lax.fori_loop(0, COMBINE_ROWS, issue, 0)

    def drain(r, carry):
        _row_copy(y_ref, buf_ref.at[0], sem, 0, r).wait()
        _row_copy(y_ref, buf_ref.at[1], sem, 0, r).wait()
        return carry

    lax.fori_loop(0, COMBINE_ROWS, drain, 0)
    wts = wts_ref[...]
    f = buf_ref[0] * wts[:, 0:1] + buf_ref[1] * wts[:, 1:2]
    o_ref[...] = _residual_norm(h_ref[...], f, gate_ref[...], lg_ref[...], lb_ref[...])


def _moe_combine(geom, h, y_sorted, pos, wts, mods, gate_slot, ln_g, ln_b):
    rows = geom.n_lat
    steps = rows // COMBINE_ROWS
    per_mod = ROW_TILE // COMBINE_ROWS
    row = pl.BlockSpec((COMBINE_ROWS, D_MODEL), lambda i: (i, 0))
    vec = pl.BlockSpec((1, D_MODEL), lambda i: (0, 0))
    return pl.pallas_call(
        _combine_kernel,
        grid=(steps,),
        in_specs=[pl.BlockSpec((1, 1, 2 * COMBINE_ROWS), lambda i: (i, 0, 0), memory_space=pltpu.SMEM),
                  row,
                  pl.BlockSpec((COMBINE_ROWS, LANES), lambda i: (i, 0)),
                  pl.BlockSpec((None, 1, D_MODEL), lambda i: (geom.mod_row(i // per_mod) * 6 + gate_slot, 0, 0)),
                  vec, vec,
                  pl.BlockSpec(memory_space=pl.ANY)],
        out_specs=row,
        out_shape=jax.ShapeDtypeStruct((rows, D_MODEL), F32),
        scratch_shapes=[pltpu.VMEM((2, COMBINE_ROWS, D_MODEL), F32), pltpu.SemaphoreType.DMA(())],
        compiler_params=_cparams(("arbitrary",)),
        name="moe_combine",
    )(pos.reshape(steps, 1, 2 * COMBINE_ROWS), h, wts, mods, ln_g.reshape(1, D_MODEL),
      ln_b.reshape(1, D_MODEL), y_sorted)


def _route(top_idx, n_tokens):
    flat = top_idx.reshape(-1)
    onehot = (flat[:, None] == jnp.arange(N_EXPERTS, dtype=jnp.int32)[None, :]).astype(jnp.int32)
    csum = jnp.cumsum(onehot, axis=0)
    rank = jnp.sum(csum * onehot, axis=1) - 1
    counts = csum[-1]
    tiles_per = (counts + MOE_TM - 1) // MOE_TM
    tile_end = jnp.cumsum(tiles_per)
    tile_start = tile_end - tiles_per
    pos = (tile_start[flat] * MOE_TM + rank).astype(jnp.int32)
    n_tiles = (2 * n_tokens) // MOE_TM + N_EXPERTS
    n_used = tile_end[-1].astype(jnp.int32)
    tile_ids = jnp.arange(n_tiles, dtype=jnp.int32)
    tile_expert = jnp.sum((tile_ids[:, None] >= tile_end[None, :]).astype(jnp.int32), axis=1)
    tile_expert = jnp.minimum(tile_expert, N_EXPERTS - 1).astype(jnp.int32)
    slot_token = jnp.zeros((n_tiles * MOE_TM,), jnp.int32).at[pos].set(
        jnp.arange(2 * n_tokens, dtype=jnp.int32) // 2)
    return pos, slot_token, tile_expert, n_used.reshape(1)


def kernel(x, c, ctx, c_ctx, ada_w, ada_b, w_in, mamba_conv_w, mamba_conv_b, dt_bias_fwd, dt_bias_bwd, a_log_fwd, a_log_bwd, d_skip, ssm_norm_w, conf_conv_w, conf_conv_b, conf_ln_g, conf_ln_b, w_out, ln1_g, ln1_b, ln2_g, ln2_b, ffn_w1, ffn_w3, ffn_w2, router_w, router_b, moe_w1, moe_w3, moe_w2):
    bsz, seq, d = x.shape
    ctx_len = ctx.shape[1]
    geom = _Geom(bsz, seq, ctx_len)
    n_all, n_lat = geom.n_all, geom.n_lat

    mod_rows = SUBLANES
    cond_rows = jnp.zeros((mod_rows, d), F32).at[:bsz].set(c).at[bsz].set(c_ctx)
    mods_all = _ada_mods(cond_rows, ada_w, ada_b).reshape(DEPTH, mod_rows * 6, 1, d)

    h, u = _pack_modulate(geom, x.reshape(n_lat, d), ctx.reshape(bsz * ctx_len, d), mods_all[0])

    tm_all = _pick_tile(n_all, 1056)
    tm_lat = _pick_tile(n_lat, 1024)

    for i in range(DEPTH):
        mods = mods_all[i]
        last = i == DEPTH - 1
        w_main = jnp.concatenate([w_in[i][:, :OFF_DT], w_in[i][:, OFF_X:]], axis=1).astype(BF16)
        w_dt = jnp.pad(w_in[i][:, OFF_DT:OFF_X], ((0, 0), (0, LANES - 2 * SSM_HEADS))).astype(BF16)
        w_o = w_out[i].astype(BF16)

        proj = _matmul([(u, w_main)], n_all, tm_all, 1024, F32, "in_proj")
        dtr = _matmul([(u, w_dt)], n_all, tm_all, LANES, F32, "dt_proj")
        dtt = jnp.transpose(dtr[:, :2 * SSM_HEADS])

        xs, bc = _ssm_conv(geom, proj, mamba_conv_w[i], mamba_conv_b[i])
        y_b = _ssd_pass(geom, xs, bc, dtr, dtt, dt_bias_bwd[i], a_log_bwd[i], rev=True)
        dsk = jnp.repeat(d_skip[i], SSM_HEADDIM).reshape(1, D_SSM)
        y_ssm = _ssd_pass(geom, xs, bc, dtr, dtt, dt_bias_fwd[i], a_log_fwd[i], rev=False,
                          final_args=(y_b, proj, dsk, ssm_norm_w[i].reshape(1, D_SSM)))

        if i % 2 == 0:
            v = _conformer_rows(geom, proj, conf_conv_w[i], conf_conv_b[i], conf_ln_g[i], conf_ln_b[i])
        else:
            conv = _conformer_cols_conv(geom, proj, conf_conv_w[i], conf_conv_b[i])
            v = _ln_swish(conv, conf_ln_g[i], conf_ln_b[i])

        rows = n_lat if last else n_all
        tm = tm_lat if last else tm_all
        mix = _matmul([(y_ssm, w_o[:D_SSM]), (v, w_o[D_SSM:])], rows, tm, 1024, F32, "out_proj")

        j = i // 2
        if i % 2 == 0:
            h, u2 = _epilogue(geom, h, mix, rows, mods, 2, ln1_g[i], ln1_b[i], mods, (3, 4))
            w1 = jnp.pad(ffn_w1[j], ((0, 0), (0, FFN_PAD - FFN_DENSE))).astype(BF16)
            w3 = jnp.pad(ffn_w3[j], ((0, 0), (0, FFN_PAD - FFN_DENSE))).astype(BF16)
            w2 = jnp.pad(ffn_w2[j], ((0, FFN_PAD - FFN_DENSE), (0, 0))).astype(BF16)
            hid = _gated_up(u2, w1, w3, rows, tm, 512)
            f = _matmul_ktiled(hid, w2, rows, tm, 2048, 1024, "ffn_down")
            if last:
                h = _epilogue(geom, h, f, rows, mods, 5, ln2_g[i], ln2_b[i])[0]
            else:
                h, u = _epilogue(geom, h, f, rows, mods, 5, ln2_g[i], ln2_b[i], mods_all[i + 1], (0, 1))
        else:
            rw = jnp.pad(router_w[j], ((0, 0), (0, LANES - N_EXPERTS)))
            rw_hi = rw.astype(BF16)
            rw_lo = (rw - rw_hi.astype(F32)).astype(BF16)
            rb = jnp.pad(router_b[j], (0, LANES - N_EXPERTS)).reshape(1, LANES)
            assert last, "a routed layer is only supported as the final layer"
            h, u2f, top_idx, top_w = _epilogue(geom, h, mix, rows, mods, 2, ln1_g[i], ln1_b[i], mods, (3, 4),
                                                  router=(rw_hi, rw_lo, rb))
            pos, slot_token, tile_expert, n_used = _route(top_idx[:, :2], rows)
            a_sorted = _gather_rows(u2f, slot_token)
            hid = _moe_up(a_sorted, moe_w1[j].astype(BF16), moe_w3[j].astype(BF16), tile_expert, n_used)
            y_sorted = _moe_down(hid, moe_w2[j].astype(BF16), tile_expert, n_used)
            h = _moe_combine(geom, h, y_sorted, pos, top_w, mods, 5, ln2_g[i], ln2_b[i])

    return h[:n_lat].reshape(bsz, seq, d)
```

```python
import functools

import jax
import jax.numpy as jnp
from jax import lax
from jax.experimental import pallas as pl
from jax.experimental.pallas import tpu as pltpu

F32 = jnp.float32
BF16 = jnp.bfloat16

D_MODEL = 4096
DEPTH = 2
GRID_W = 64
D_SSM = 2048
D_CONF = 2048
SSM_HEADDIM = 64
SSM_HEADS = 32
SSM_GROUPS = 8
HEADS_PER_GROUP = 4
SSM_STATE = 128
SSM_CONV = 5
CHUNK = 128
GROUP_W = HEADS_PER_GROUP * SSM_HEADDIM
CONF_WIDTH = 31
CONF_HALF = CONF_WIDTH // 2
OFF_DT = D_SSM
OFF_X = OFF_DT + 2 * SSM_HEADS
N_MAIN = 10240
COL_X, COL_BC, COL_VAL, COL_GATE = 1, 2, 3, 4
FFN_DENSE = 11008
FFN_PAD = 11264
N_EXPERTS = 8
FFN_EXPERT = 3584
LN_EPS = 1e-5
ALPHA = (2 * DEPTH) ** 0.25

LANES = 128
SUBLANES = 8
ROW_TILE = 256
MOE_TM = 512
VMEM_LIMIT = 56 * 1024 * 1024


def _cparams(sem, vmem=VMEM_LIMIT):
    return pltpu.CompilerParams(dimension_semantics=sem, vmem_limit_bytes=vmem)


def _sigmoid(x):
    return 1.0 / (1.0 + jnp.exp(-x))


def _silu(x):
    return x * _sigmoid(x)


def _softplus(x):
    return jnp.maximum(x, 0.0) + jnp.log(1.0 + jnp.exp(-jnp.abs(x)))


def _standardize(x):
    mu = jnp.mean(x, axis=-1, keepdims=True)
    xc = x - mu
    var = jnp.mean(xc * xc, axis=-1, keepdims=True)
    return xc * lax.rsqrt(var + LN_EPS)


def _pick_tile(n, target, mult=16):
    best = None
    for t in range(mult, min(n, target) + 1, mult):
        if n % t == 0:
            best = t
    assert best is not None, (n, target)
    return best


def _ada_kernel(c_ref, w_ref, b_ref, o_ref):
    cond = _silu(c_ref[...]).astype(BF16)
    o_ref[...] = jnp.dot(cond, w_ref[...].astype(BF16), preferred_element_type=F32) + b_ref[...]


def _ada_mods(cond_rows, ada_w, ada_b):
    depth, d, n = ada_w.shape
    rows = cond_rows.shape[0]
    tn = 512
    return pl.pallas_call(
        _ada_kernel,
        grid=(depth, n // tn),
        in_specs=[
            pl.BlockSpec((rows, d), lambda l, j: (0, 0)),
            pl.BlockSpec((None, d, tn), lambda l, j: (l, 0, j)),
            pl.BlockSpec((None, 1, tn), lambda l, j: (l, 0, j)),
        ],
        out_specs=pl.BlockSpec((None, rows, tn), lambda l, j: (l, 0, j)),
        out_shape=jax.ShapeDtypeStruct((depth, rows, n), F32),
        compiler_params=_cparams(("arbitrary", "arbitrary")),
        name="ada_mods",
    )(cond_rows, ada_w, ada_b.reshape(depth, 1, n))


class _Geom:
    def __init__(self, bsz, seq, ctx_len):
        self.bsz, self.seq, self.ctx = bsz, seq, ctx_len
        self.n_lat = bsz * seq
        self.n_all = self.n_lat + bsz * ctx_len
        assert seq % ROW_TILE == 0 and ctx_len % ROW_TILE == 0 and seq % GRID_W == 0
        self.lat_tiles = self.n_lat // ROW_TILE
        self.tiles_per_seq = seq // ROW_TILE
        self.tiles_per_ctx = ctx_len // ROW_TILE

    def mod_row(self, i):
        return jnp.where(i < self.lat_tiles, i // self.tiles_per_seq, self.bsz)

    def seq_edges(self, i):
        lat = i < self.lat_tiles
        j = i - self.lat_tiles
        first = jnp.where(lat, i % self.tiles_per_seq == 0, j % self.tiles_per_ctx == 0)
        last = jnp.where(lat, i % self.tiles_per_seq == self.tiles_per_seq - 1,
                         j % self.tiles_per_ctx == self.tiles_per_ctx - 1)
        return first, last


def _mod_spec(geom, slot):
    return pl.BlockSpec((None, 1, D_MODEL), lambda i: (geom.mod_row(i) * 6 + slot, 0, 0))


def _pack_mod_kernel(x_ref, c_ref, sh_ref, sc_ref, h_ref, u_ref, *, lat_tiles):
    i = pl.program_id(0)

    def emit(v):
        h_ref[...] = v
        u_ref[...] = (_standardize(v) * (1.0 + sc_ref[...]) + sh_ref[...]).astype(BF16)

    @pl.when(i < lat_tiles)
    def _():
        emit(x_ref[...])

    @pl.when(i >= lat_tiles)
    def _():
        emit(c_ref[...])


def _pack_modulate(geom, x2, ctx2, mods):
    lt = geom.lat_tiles
    n_tiles = geom.n_all // ROW_TILE
    row = pl.BlockSpec((ROW_TILE, D_MODEL), lambda i: (i, 0))
    return pl.pallas_call(
        functools.partial(_pack_mod_kernel, lat_tiles=lt),
        grid=(n_tiles,),
        in_specs=[
            pl.BlockSpec((ROW_TILE, D_MODEL), lambda i: (jnp.minimum(i, lt - 1), 0)),
            pl.BlockSpec((ROW_TILE, D_MODEL), lambda i: (jnp.maximum(i - lt, 0), 0)),
            _mod_spec(geom, 0), _mod_spec(geom, 1),
        ],
        out_specs=[row, row],
        out_shape=[jax.ShapeDtypeStruct((geom.n_all, D_MODEL), F32),
                   jax.ShapeDtypeStruct((geom.n_all, D_MODEL), BF16)],
        compiler_params=_cparams(("arbitrary",)),
        name="pack_modulate",
    )(x2, ctx2, mods, mods)


def _mm_kernel(*refs, n_pairs):
    o_ref = refs[-1]
    acc = jnp.dot(refs[0][...], refs[1][...], preferred_element_type=F32)
    for p in range(1, n_pairs):
        acc = acc + jnp.dot(refs[2 * p][...], refs[2 * p + 1][...], preferred_element_type=F32)
    o_ref[...] = acc.astype(o_ref.dtype)


def _matmul(pairs, rows, tm, tn, out_dtype, name):
    n = pairs[0][1].shape[1]
    assert rows % tm == 0 and n % tn == 0
    in_specs, args = [], []
    for a, w in pairs:
        k = a.shape[1]
        in_specs += [pl.BlockSpec((tm, k), lambda i, j: (i, 0)), pl.BlockSpec((k, tn), lambda i, j: (0, j))]
        args += [a, w]
    return pl.pallas_call(
        functools.partial(_mm_kernel, n_pairs=len(pairs)),
        grid=(rows // tm, n // tn),
        in_specs=in_specs,
        out_specs=pl.BlockSpec((tm, tn), lambda i, j: (i, j)),
        out_shape=jax.ShapeDtypeStruct((rows, n), out_dtype),
        compiler_params=_cparams(("arbitrary", "arbitrary")),
        name=name,
    )(*args)


def _gated_up_kernel(a_ref, w1_ref, w3_ref, o_ref):
    a = a_ref[...]
    h1 = jnp.dot(a, w1_ref[...], preferred_element_type=F32)
    h3 = jnp.dot(a, w3_ref[...], preferred_element_type=F32)
    o_ref[...] = (_silu(h1) * h3).astype(o_ref.dtype)


def _gated_up(a, w1, w3, rows, tm, tn):
    k, n = w1.shape
    return pl.pallas_call(
        _gated_up_kernel,
        grid=(rows // tm, n // tn),
        in_specs=[pl.BlockSpec((tm, k), lambda i, j: (i, 0)),
                  pl.BlockSpec((k, tn), lambda i, j: (0, j)),
                  pl.BlockSpec((k, tn), lambda i, j: (0, j))],
        out_specs=pl.BlockSpec((tm, tn), lambda i, j: (i, j)),
        out_shape=jax.ShapeDtypeStruct((rows, n), BF16),
        compiler_params=_cparams(("arbitrary", "arbitrary")),
        name="ffn_up",
    )(a, w1, w3)


def _mm_acc_kernel(a_ref, w_ref, o_ref, acc_ref):
    k = pl.program_id(2)

    @pl.when(k == 0)
    def _():
        acc_ref[...] = jnp.zeros_like(acc_ref)

    acc_ref[...] += jnp.dot(a_ref[...], w_ref[...], preferred_element_type=F32)

    @pl.when(k == pl.num_programs(2) - 1)
    def _():
        o_ref[...] = acc_ref[...]


def _matmul_ktiled(a, w, rows, tm, tn, tk, name):
    kk, n = w.shape
    assert rows % tm == 0 and n % tn == 0 and kk % tk == 0
    return pl.pallas_call(
        _mm_acc_kernel,
        grid=(rows // tm, n // tn, kk // tk),
        in_specs=[pl.BlockSpec((tm, tk), lambda i, j, k: (i, k)),
                  pl.BlockSpec((tk, tn), lambda i, j, k: (k, j))],
        out_specs=pl.BlockSpec((tm, tn), lambda i, j, k: (i, j)),
        out_shape=jax.ShapeDtypeStruct((rows, n), F32),
        scratch_shapes=[pltpu.VMEM((tm, tn), F32)],
        compiler_params=_cparams(("arbitrary", "arbitrary", "arbitrary")),
        name=name,
    )(a, w)


def _conv5_kernel(xm_ref, xp_ref, xn_ref, bm_ref, bp_ref, bn_ref, w_ref, b_ref, xo_ref, bco_ref, scr_ref,
                  *, geom):
    i = pl.program_id(0)
    first, last = geom.seq_edges(i)
    half = D_SSM
    pad = SUBLANES
    for part, (m_ref, p_ref, n_ref) in enumerate(((xm_ref, xp_ref, xn_ref), (bm_ref, bp_ref, bn_ref))):
        lo = part * half
        scr_ref[0:pad, lo:lo + half] = jnp.where(first, 0.0, p_ref[...])
        scr_ref[pad:pad + ROW_TILE, lo:lo + half] = m_ref[...]
        scr_ref[pad + ROW_TILE:2 * pad + ROW_TILE, lo:lo + half] = jnp.where(last, 0.0, n_ref[...])
    rows_blk, lane_blk = 64, 512
    base = pad - SSM_CONV // 2
    for c0 in range(0, 2 * half, lane_blk):
        wc = w_ref[:, c0:c0 + lane_blk]
        bias = b_ref[:, c0:c0 + lane_blk]
        for r0 in range(0, ROW_TILE, rows_blk):
            acc = bias
            for k in range(SSM_CONV):
                acc = acc + scr_ref[base + r0 + k:base + r0 + k + rows_blk, c0:c0 + lane_blk] * wc[k:k + 1, :]
            val = _silu(acc)
            if c0 < half:
                xo_ref[r0:r0 + rows_blk, c0:c0 + lane_blk] = val
            else:
                bco_ref[r0:r0 + rows_blk, c0 - half:c0 - half + lane_blk] = val.astype(BF16)


def _ssm_conv(geom, proj, conv_w, conv_b):
    n_tiles = geom.n_all // ROW_TILE
    sub = ROW_TILE // SUBLANES
    last_blk = geom.n_all // SUBLANES - 1

    def main(col):
        return pl.BlockSpec((ROW_TILE, D_SSM), lambda i: (i, col))

    def prev(col):
        return pl.BlockSpec((SUBLANES, D_SSM), lambda i: (jnp.maximum(i * sub - 1, 0), col))

    def nxt(col):
        return pl.BlockSpec((SUBLANES, D_SSM), lambda i: (jnp.minimum((i + 1) * sub, last_blk), col))

    width = 2 * D_SSM
    return pl.pallas_call(
        functools.partial(_conv5_kernel, geom=geom),
        grid=(n_tiles,),
        in_specs=[main(COL_X), prev(COL_X), nxt(COL_X), main(COL_BC), prev(COL_BC), nxt(COL_BC),
                  pl.BlockSpec((SSM_CONV, width), lambda i: (0, 0)),
                  pl.BlockSpec((1, width), lambda i: (0, 0))],
        out_specs=[pl.BlockSpec((ROW_TILE, D_SSM), lambda i: (i, 0)),
                   pl.BlockSpec((ROW_TILE, D_SSM), lambda i: (i, 0))],
        out_shape=[jax.ShapeDtypeStruct((geom.n_all, D_SSM), F32),
                   jax.ShapeDtypeStruct((geom.n_all, D_SSM), BF16)],
        scratch_shapes=[pltpu.VMEM((ROW_TILE + 2 * SUBLANES, width), F32)],
        compiler_params=_cparams(("arbitrary",)),
        name="ssm_conv",
    )(proj, proj, proj, proj, proj, proj, conv_w, conv_b.reshape(1, width))


def _split3_dot(lhs_fn, v):
    v1 = v.astype(BF16)
    r1 = v - v1.astype(F32)
    v2 = r1.astype(BF16)
    v3 = (r1 - v2.astype(F32)).astype(BF16)
    return lhs_fn(v1) + lhs_fn(v2) + lhs_fn(v3)


def _ssd_kernel(*refs, rev, final):
    if final:
        (x_ref, bc_ref, dtr_ref, dtt_ref, pbr_ref, pbc_ref, alr_ref, alc_ref,
         yb_ref, z_ref, dsk_ref, nw_ref, o_ref, s_ref) = refs
    else:
        x_ref, bc_ref, dtr_ref, dtt_ref, pbr_ref, pbc_ref, alr_ref, alc_ref, o_ref, s_ref = refs
    step = pl.program_id(1)

    @pl.when(step == 0)
    def _():
        s_ref[...] = jnp.zeros_like(s_ref)

    n = CHUNK
    row = lax.broadcasted_iota(jnp.int32, (n, n), 0)
    col = lax.broadcasted_iota(jnp.int32, (n, n), 1)
    tri = (row <= col) if rev else (row >= col)
    cum_l = tri.astype(BF16)
    cum_r = jnp.logical_not(tri) | (row == col)
    cum_r = cum_r.astype(BF16)
    h_off = SSM_HEADS if rev else 0
    edge = 0 if rev else n - 1

    dt = _softplus(dtr_ref[...] + pbr_ref[...])
    a = -dt * jnp.exp(alr_ref[...])
    dtt = _softplus(dtt_ref[...] + pbc_ref[...])
    at = -dtt * jnp.exp(alc_ref[...])
    acum = _split3_dot(lambda v: jnp.dot(cum_l, v, preferred_element_type=F32), a)
    acum_t = _split3_dot(lambda v: jnp.dot(v, cum_r, preferred_element_type=F32), at)

    lane_lo = lax.broadcasted_iota(jnp.int32, (n, LANES), 1) < SSM_HEADDIM
    lane_w = lax.broadcasted_iota(jnp.int32, (n, GROUP_W), 1)

    for g in range(SSM_GROUPS):
        b_g = bc_ref[:, g * SSM_STATE:(g + 1) * SSM_STATE]
        c_g = bc_ref[:, SSM_GROUPS * SSM_STATE + g * SSM_STATE:SSM_GROUPS * SSM_STATE + (g + 1) * SSM_STATE]
        x_g = x_ref[:, g * GROUP_W:(g + 1) * GROUP_W]
        x_gb = x_g.astype(BF16)
        s_g = s_ref[g]
        cb = lax.dot_general(c_g, b_g, (((1,), (1,)), ((), ())), preferred_element_type=F32)

        m_parts, ecols, wcols, etots = [], [], [], []
        for e in range(HEADS_PER_GROUP):
            h = g * HEADS_PER_GROUP + e
            colb = jnp.broadcast_to(acum[:, h_off + h:h_off + h + 1], (n, n))
            rowb = jnp.broadcast_to(acum_t[h:h + 1, :], (n, n))
            dtrow = jnp.broadcast_to(dtt[h:h + 1, :], (n, n))
            dtcol = jnp.broadcast_to(dt[:, h_off + h:h_off + h + 1], (n, n))
            decay = jnp.exp(jnp.where(tri, colb - rowb, -jnp.inf))
            m_parts.append((cb * decay * dtrow).astype(BF16))
            ecol = jnp.exp(colb)
            tot = jnp.broadcast_to(colb[edge:edge + 1, :], (n, n))
            ecols.append(ecol)
            wcols.append(jnp.exp(tot - colb) * dtcol)
            etots.append(ecol[edge:edge + 1, :])

        def per_head(parts):
            return jnp.concatenate([jnp.where(lane_lo[:parts[0].shape[0]], parts[0], parts[1]),
                                    jnp.where(lane_lo[:parts[0].shape[0]], parts[2], parts[3])], axis=1)

        scale_e = per_head(ecols)
        scale_w = per_head(wcols)
        scale_t = per_head(etots)

        y = None
        for pair in range(HEADS_PER_GROUP // 2):
            e0, e1 = 2 * pair, 2 * pair + 1
            lhs = jnp.concatenate([m_parts[e0], m_parts[e1]], axis=1)
            r0 = jnp.where((lane_w // SSM_HEADDIM) == e0, x_gb, jnp.zeros_like(x_gb))
            r1 = jnp.where((lane_w // SSM_HEADDIM) == e1, x_gb, jnp.zeros_like(x_gb))
            part = jnp.dot(lhs, jnp.concatenate([r0, r1], axis=0), preferred_element_type=F32)
            y = part if y is None else y + part
        y = y + jnp.dot(c_g, s_g.astype(BF16), preferred_element_type=F32) * scale_e
        xw = (x_g * scale_w).astype(BF16)
        b_t = jnp.transpose(b_g.astype(F32)).astype(BF16)
        s_ref[g] = s_g * scale_t + jnp.dot(b_t, xw, preferred_element_type=F32)

        cols = slice(g * GROUP_W, (g + 1) * GROUP_W)
        if final:
            y = y + yb_ref[:, cols] + x_g * dsk_ref[:, cols]
            gz = y * _silu(z_ref[:, cols])
            ms = jnp.mean(gz * gz, axis=-1, keepdims=True)
            o_ref[:, cols] = (gz * lax.rsqrt(ms + LN_EPS) * nw_ref[:, cols]).astype(o_ref.dtype)
        else:
            o_ref[:, cols] = y


def _ssd_pass(geom, x, bc, dtr, dtt, dt_bias, a_log, rev, final_args=None):
    ncl = geom.seq // CHUNK
    ncc = geom.ctx // CHUNK
    lat_chunks = geom.n_lat // CHUNK

    def chunk(b, s):
        if rev:
            return jnp.where(s < ncc, lat_chunks + b * ncc + (ncc - 1 - s), b * ncl + (ncl - 1 - (s - ncc)))
        return jnp.where(s < ncc, lat_chunks + b * ncc + s, b * ncl + (s - ncc))

    def rows(width):
        return pl.BlockSpec((CHUNK, width), lambda b, s: (chunk(b, s), 0))

    d_idx = 1 if rev else 0
    small = lambda shape: pl.BlockSpec(shape, lambda b, s: (0, 0))
    in_specs = [rows(D_SSM), rows(D_SSM), rows(LANES),
                pl.BlockSpec((SSM_HEADS, CHUNK), lambda b, s: (d_idx, chunk(b, s))),
                small((1, LANES)), small((SSM_HEADS, 1)), small((1, LANES)), small((SSM_HEADS, 1))]
    lane_pad = (d_idx * SSM_HEADS, LANES - (d_idx + 1) * SSM_HEADS)
    args = [x, bc, dtr, dtt, jnp.pad(dt_bias, lane_pad).reshape(1, LANES), dt_bias.reshape(SSM_HEADS, 1),
            jnp.pad(a_log, lane_pad).reshape(1, LANES), a_log.reshape(SSM_HEADS, 1)]
    final = final_args is not None
    if final:
        y_other, proj, dsk, nw = final_args
        in_specs += [rows(D_SSM), rows(D_SSM), small((1, D_SSM)), small((1, D_SSM))]
        args += [y_other, proj, dsk, nw]
    return pl.pallas_call(
        functools.partial(_ssd_kernel, rev=rev, final=final),
        grid=(geom.bsz, ncc + ncl),
        in_specs=in_specs,
        out_specs=rows(D_SSM),
        out_shape=jax.ShapeDtypeStruct((geom.n_all, D_SSM), BF16 if final else F32),
        scratch_shapes=[pltpu.VMEM((SSM_GROUPS, SSM_STATE, GROUP_W), F32)],
        compiler_params=_cparams(("arbitrary", "arbitrary")),
        name="ssd_fwd" if final else "ssd_bwd",
    )(*args)


CONF_LANE_BLK = 512
CONF_SLOT = 2048


def _shift_matrix(run_len):
    l = jnp.arange(run_len)[:, None, None]
    k = jnp.arange(CONF_WIDTH)[None, :, None]
    j = jnp.arange(run_len)[None, None, :]
    return (j == l + k - CONF_HALF).astype(BF16).reshape(run_len, CONF_WIDTH * run_len)


def _conf_rows_kernel(val_ref, gate_ref, s_run_ref, s_ctx_ref, w_ref, cb_ref, g_ref, b_ref, o_ref, rhs_ref, res_ref,
                      *, lat_tiles):
    i = pl.program_id(0)

    def run_conv(run_len, n_runs, s_ref, slots):
        span = CONF_WIDTH * run_len
        it = 0
        for c0 in range(0, D_CONF, CONF_LANE_BLK):
            lanes = slice(c0, c0 + CONF_LANE_BLK)
            for r in range(n_runs):
                rows = slice(r * run_len, (r + 1) * run_len)
                base = (it % slots) * CONF_SLOT
                it += 1
                glu = (val_ref[rows, lanes] * _sigmoid(gate_ref[rows, lanes])).astype(BF16)
                for k in range(CONF_WIDTH):
                    rhs_ref[base + k * run_len:base + (k + 1) * run_len, :] = (
                        glu * w_ref[k:k + 1, lanes].astype(BF16))
                res_ref[rows, lanes] = (jnp.dot(s_ref[...], rhs_ref[base:base + span, :],
                                                preferred_element_type=F32) + cb_ref[:, lanes])

    @pl.when(i < lat_tiles)
    def _():
        run_conv(GRID_W, ROW_TILE // GRID_W, s_run_ref, 2)

    @pl.when(i >= lat_tiles)
    def _():
        run_conv(ROW_TILE, 1, s_ctx_ref, 1)

    rows_blk = 64
    for r0 in range(0, ROW_TILE, rows_blk):
        v = _standardize(res_ref[r0:r0 + rows_blk, :]) * g_ref[...] + b_ref[...]
        o_ref[r0:r0 + rows_blk, :] = _silu(v).astype(o_ref.dtype)


def _conformer_rows(geom, proj, w, cb, ln_g, ln_b):
    assert geom.ctx == ROW_TILE and CONF_SLOT >= CONF_WIDTH * GRID_W
    n_tiles = geom.n_all // ROW_TILE
    full = lambda shape: pl.BlockSpec(shape, lambda i: (0,) * len(shape))
    return pl.pallas_call(
        functools.partial(_conf_rows_kernel, lat_tiles=geom.lat_tiles),
        grid=(n_tiles,),
        in_specs=[pl.BlockSpec((ROW_TILE, D_CONF), lambda i: (i, COL_VAL)),
                  pl.BlockSpec((ROW_TILE, D_CONF), lambda i: (i, COL_GATE)),
                  full((GRID_W, CONF_WIDTH * GRID_W)), full((ROW_TILE, CONF_WIDTH * ROW_TILE)),
                  full((CONF_WIDTH, D_CONF)), full((1, D_CONF)), full((1, D_CONF)), full((1, D_CONF))],
        out_specs=pl.BlockSpec((ROW_TILE, D_CONF), lambda i: (i, 0)),
        out_shape=jax.ShapeDtypeStruct((geom.n_all, D_CONF), BF16),
        scratch_shapes=[pltpu.VMEM((CONF_WIDTH * ROW_TILE, CONF_LANE_BLK), BF16),
                        pltpu.VMEM((ROW_TILE, D_CONF), F32)],
        compiler_params=_cparams(("arbitrary",)),
        name="conformer_rows",
    )(proj, proj, _shift_matrix(GRID_W), _shift_matrix(ROW_TILE), w, cb.reshape(1, D_CONF),
      ln_g.reshape(1, D_CONF), ln_b.reshape(1, D_CONF))


def _conf_cols_kernel(val_ref, gate_ref, w_ref, cb_ref, o_ref, scr_ref, *, seq):
    halo = CONF_HALF * GRID_W
    scr_ref[0:halo, :] = jnp.zeros((halo, LANES), F32)
    scr_ref[halo + seq:2 * halo + seq, :] = jnp.zeros((halo, LANES), F32)
    blk = 256
    for r0 in range(0, seq, blk):
        scr_ref[halo + r0:halo + r0 + blk, :] = val_ref[r0:r0 + blk, :] * _sigmoid(gate_ref[r0:r0 + blk, :])
    rows_blk = 128

    def body(rb, carry):
        acc = jnp.broadcast_to(cb_ref[...], (rows_blk, LANES))
        for k in range(CONF_WIDTH):
            start = pl.multiple_of(rb * rows_blk + k * GRID_W, GRID_W)
            acc = acc + scr_ref[pl.ds(start, rows_blk), :] * w_ref[k:k + 1, :]
        o_ref[pl.ds(pl.multiple_of(rb * rows_blk, rows_blk), rows_blk), :] = acc
        return carry

    lax.fori_loop(0, seq // rows_blk, body, 0)


def _conformer_cols_conv(geom, proj, w, cb):
    seq = geom.seq
    nblk = D_CONF // LANES
    val0 = COL_VAL * D_CONF // LANES
    gate0 = COL_GATE * D_CONF // LANES
    return pl.pallas_call(
        functools.partial(_conf_cols_kernel, seq=seq),
        grid=(geom.bsz, nblk),
        in_specs=[pl.BlockSpec((seq, LANES), lambda b, c: (b, val0 + c)),
                  pl.BlockSpec((seq, LANES), lambda b, c: (b, gate0 + c)),
                  pl.BlockSpec((CONF_WIDTH, LANES), lambda b, c: (0, c)),
                  pl.BlockSpec((1, LANES), lambda b, c: (0, c))],
        out_specs=pl.BlockSpec((seq, LANES), lambda b, c: (b, c)),
        out_shape=jax.ShapeDtypeStruct((geom.n_lat, D_CONF), F32),
        scratch_shapes=[pltpu.VMEM((seq + 2 * CONF_HALF * GRID_W, LANES), F32)],
        compiler_params=_cparams(("arbitrary", "arbitrary")),
        name="conformer_cols",
    )(proj, proj, w, cb.reshape(1, D_CONF))


def _ln_swish_kernel(v_ref, g_ref, b_ref, o_ref):
    o_ref[...] = _silu(_standardize(v_ref[...]) * g_ref[...] + b_ref[...]).astype(o_ref.dtype)


def _ln_swish(conv, ln_g, ln_b):
    rows = conv.shape[0]
    full = pl.BlockSpec((1, D_CONF), lambda i: (0, 0))
    return pl.pallas_call(
        _ln_swish_kernel,
        grid=(rows // ROW_TILE,),
        in_specs=[pl.BlockSpec((ROW_TILE, D_CONF), lambda i: (i, 0)), full, full],
        out_specs=pl.BlockSpec((ROW_TILE, D_CONF), lambda i: (i, 0)),
        out_shape=jax.ShapeDtypeStruct((rows, D_CONF), BF16),
        compiler_params=_cparams(("arbitrary",)),
        name="conformer_ln",
    )(conv, ln_g.reshape(1, D_CONF), ln_b.reshape(1, D_CONF))


def _top2(logits):
    lane_i = lax.broadcasted_iota(jnp.int32, logits.shape, 1)
    lane = lane_i.astype(F32)
    m1 = jnp.max(logits, axis=-1, keepdims=True)
    i1 = jnp.min(jnp.where(logits == m1, lane, float(LANES)), axis=-1, keepdims=True)
    rest = jnp.where(lane == i1, -jnp.inf, logits)
    m2 = jnp.max(rest, axis=-1, keepdims=True)
    i2 = jnp.min(jnp.where(rest == m2, lane, float(LANES)), axis=-1, keepdims=True)
    e2 = jnp.exp(m2 - m1)
    w1 = 1.0 / (1.0 + e2)
    w2 = e2 / (1.0 + e2)
    idx = jnp.where(lane_i == 0, i1, jnp.where(lane_i == 1, i2, 0.0)).astype(jnp.int32)
    wts = jnp.where(lane_i == 0, w1, jnp.where(lane_i == 1, w2, 0.0))
    return idx, wts


HALF_D = D_MODEL // 2
_HI_MASK = 0xFFFF0000


def _pack_bf16_pairs(u):
    bits = pltpu.bitcast(u.astype(BF16).astype(F32), jnp.uint32)
    return (bits[:, HALF_D:] & jnp.uint32(_HI_MASK)) | (bits[:, :HALF_D] >> 16)


def _unpack_bf16_pairs(p):
    lo = pltpu.bitcast(p << 16, F32).astype(BF16)
    hi = pltpu.bitcast(p & jnp.uint32(_HI_MASK), F32).astype(BF16)
    return lo, hi


def _residual_norm(h, f, gate, ln_g, ln_b):
    return _standardize(ALPHA * h + gate * f) * ln_g + ln_b


def _epilogue_kernel(*refs, modulate, router):
    h_ref, f_ref, gate_ref, lg_ref, lb_ref = refs[:5]
    pos = 5
    if modulate:
        sh_ref, sc_ref = refs[pos:pos + 2]
        pos += 2
    if router:
        rwh_ref, rwl_ref, rb_ref = refs[pos:pos + 3]
        pos += 3
    outs = refs[pos:]
    hn = _residual_norm(h_ref[...], f_ref[...], gate_ref[...], lg_ref[...], lb_ref[...])
    outs[0][...] = hn
    if modulate:
        u = _standardize(hn) * (1.0 + sc_ref[...]) + sh_ref[...]
        if router:
            outs[1][...] = _pack_bf16_pairs(u)
        else:
            outs[1][...] = u.astype(BF16)
    if router:
        u_hi = u.astype(BF16)
        u_lo = (u - u_hi.astype(F32)).astype(BF16)
        logits = (jnp.dot(u_hi, rwh_ref[...], preferred_element_type=F32)
                  + jnp.dot(u_lo, rwh_ref[...], preferred_element_type=F32)
                  + jnp.dot(u_hi, rwl_ref[...], preferred_element_type=F32)) + rb_ref[...]
        lane = lax.broadcasted_iota(jnp.int32, logits.shape, 1)
        logits = jnp.where(lane < N_EXPERTS, logits, -jnp.inf)
        idx, wts = _top2(logits)
        outs[2][...] = idx
        outs[3][...] = wts


def _epilogue(geom, h, f, rows, mods, gate_slot, ln_g, ln_b, next_mods=None, next_slots=None, router=None):
    n_tiles = rows // ROW_TILE
    row = pl.BlockSpec((ROW_TILE, D_MODEL), lambda i: (i, 0))
    vec = pl.BlockSpec((1, D_MODEL), lambda i: (0, 0))
    in_specs = [row, row, _mod_spec(geom, gate_slot), vec, vec]
    args = [h, f, mods, ln_g.reshape(1, D_MODEL), ln_b.reshape(1, D_MODEL)]
    out_specs = [row]
    out_shape = [jax.ShapeDtypeStruct((rows, D_MODEL), F32)]
    modulate = next_mods is not None
    if modulate:
        in_specs += [_mod_spec(geom, next_slots[0]), _mod_spec(geom, next_slots[1])]
        args += [next_mods, next_mods]
        if router is not None:
            out_specs.append(pl.BlockSpec((ROW_TILE, HALF_D), lambda i: (i, 0)))
            out_shape.append(jax.ShapeDtypeStruct((rows, HALF_D), jnp.uint32))
        else:
            out_specs.append(row)
            out_shape.append(jax.ShapeDtypeStruct((rows, D_MODEL), BF16))
    if router is not None:
        rw_hi, rw_lo, rb = router
        small = pl.BlockSpec((D_MODEL, LANES), lambda i: (0, 0))
        in_specs += [small, small, pl.BlockSpec((1, LANES), lambda i: (0, 0))]
        args += [rw_hi, rw_lo, rb]
        lane_blk = pl.BlockSpec((ROW_TILE, LANES), lambda i: (i, 0))
        out_specs += [lane_blk, lane_blk]
        out_shape += [jax.ShapeDtypeStruct((rows, LANES), jnp.int32),
                      jax.ShapeDtypeStruct((rows, LANES), F32)]
    return pl.pallas_call(
        functools.partial(_epilogue_kernel, modulate=modulate, router=router is not None),
        grid=(n_tiles,),
        in_specs=in_specs,
        out_specs=out_specs,
        out_shape=out_shape,
        compiler_params=_cparams(("arbitrary",)),
        name="epilogue",
    )(*args)


GATHER_ROWS = 256


def _row_copy(src_hbm, dst_vmem, sem, src_row, dst_row):
    return pltpu.make_async_copy(src_hbm.at[pl.ds(src_row, 1), :], dst_vmem.at[pl.ds(dst_row, 1), :], sem)


def _issue_rows(src_ref, dst_refs, sem, idx_fn, n_rows):
    def body(r, carry):
        for p, dst in enumerate(dst_refs):
            _row_copy(src_ref, dst, sem, idx_fn(len(dst_refs) * r + p), r).start(priority=p)
        return carry

    lax.fori_loop(0, n_rows, body, 0)


def _drain_rows(src_ref, dst_refs, sem, n_rows):
    def body(r, carry):
        for dst in dst_refs:
            _row_copy(src_ref, dst, sem, 0, r).wait()
        return carry

    lax.fori_loop(0, n_rows, body, 0)


def _double_buffered_rows(src_ref, bufs_of_slot, sem, cur_idx_ref, nxt_idx_ref, n_rows):
    i = pl.program_id(0)
    slot = i % 2

    @pl.when(i == 0)
    def _():
        _issue_rows(src_ref, bufs_of_slot(0), sem.at[0], lambda q: cur_idx_ref[0, 0, q], n_rows)

    @pl.when(i + 1 < pl.num_programs(0))
    def _():
        _issue_rows(src_ref, bufs_of_slot(1 - slot), sem.at[1 - slot], lambda q: nxt_idx_ref[0, 0, q], n_rows)

    _drain_rows(src_ref, bufs_of_slot(slot), sem.at[slot], n_rows)
    return slot


def _idx_specs(per_step, steps):
    cur = pl.BlockSpec((1, 1, per_step), lambda i: (i, 0, 0), memory_space=pltpu.SMEM)
    nxt = pl.BlockSpec((1, 1, per_step), lambda i: (jnp.minimum(i + 1, steps - 1), 0, 0), memory_space=pltpu.SMEM)
    return cur, nxt


def _gather_kernel(idx_ref, nxt_ref, src_ref, o_ref, buf_ref, sem):
    half = GATHER_ROWS // 2

    def bufs(slot):
        return (buf_ref.at[slot, 0], buf_ref.at[slot, 1])

    slot = _double_buffered_rows(src_ref, bufs, sem, idx_ref, nxt_ref, half)
    o_ref[0] = buf_ref[slot, 0]
    o_ref[1] = buf_ref[slot, 1]


def _gather_rows(src, slot_token):
    n_slots = slot_token.shape[0]
    width = src.shape[1]
    steps = n_slots // GATHER_ROWS
    half = GATHER_ROWS // 2
    idx = jnp.transpose(slot_token.reshape(steps, 2, half), (0, 2, 1)).reshape(steps, 1, GATHER_ROWS)
    cur, nxt = _idx_specs(GATHER_ROWS, steps)
    out = pl.pallas_call(
        _gather_kernel,
        grid=(steps,),
        in_specs=[cur, nxt, pl.BlockSpec(memory_space=pl.ANY)],
        out_specs=pl.BlockSpec((None, 2, half, width), lambda i: (i, 0, 0, 0)),
        out_shape=jax.ShapeDtypeStruct((steps, 2, half, width), src.dtype),
        scratch_shapes=[pltpu.VMEM((2, 2, half, width), src.dtype), pltpu.SemaphoreType.DMA((2,))],
        compiler_params=_cparams(("arbitrary",)),
        name="moe_gather",
    )(idx, idx, src)
    return out.reshape(n_slots, width)


def _tile_idx(w, nused_ref):
    return jnp.minimum(w, nused_ref[0] - 1)


def _fresh_expert(te_ref, w):
    return (w == 0) | (te_ref[w] != te_ref[jnp.maximum(w - 1, 0)])


def _moe_up_kernel(te_ref, nused_ref, a_ref, w1_ref, w3_ref, o_ref, w1b_ref, w3b_ref):
    w = pl.program_id(1)
    used = w < nused_ref[0]

    @pl.when(used & _fresh_expert(te_ref, w))
    def _():
        w1b_ref[...] = w1_ref[...].astype(BF16)
        w3b_ref[...] = w3_ref[...].astype(BF16)

    @pl.when(used)
    def _():
        lo, hi = _unpack_bf16_pairs(a_ref[...])

        def proj(wb_ref):
            return (jnp.dot(lo, wb_ref[:HALF_D, :], preferred_element_type=F32)
                    + jnp.dot(hi, wb_ref[HALF_D:, :], preferred_element_type=F32))

        o_ref[...] = (_silu(proj(w1b_ref)) * proj(w3b_ref)).astype(o_ref.dtype)

    @pl.when(jnp.logical_not(used))
    def _():
        o_ref[...] = jnp.zeros_like(o_ref)


def _moe_up(a_packed, w1, w3, tile_expert, n_used, tf=512):
    n_slots = a_packed.shape[0]
    n_tiles = n_slots // MOE_TM
    k, f = w1.shape[1:]
    w_map = lambda j, w, te, nu: (te[_tile_idx(w, nu)], 0, j)
    grid_spec = pltpu.PrefetchScalarGridSpec(
        num_scalar_prefetch=2,
        grid=(f // tf, n_tiles),
        in_specs=[pl.BlockSpec((MOE_TM, HALF_D), lambda j, w, te, nu: (_tile_idx(w, nu), 0)),
                  pl.BlockSpec((None, k, tf), w_map),
                  pl.BlockSpec((None, k, tf), w_map, pipeline_mode=pl.Buffered(1))],
        out_specs=pl.BlockSpec((MOE_TM, tf), lambda j, w, te, nu: (w, j)),
        scratch_shapes=[pltpu.VMEM((k, tf), BF16), pltpu.VMEM((k, tf), BF16)],
    )
    return pl.pallas_call(
        _moe_up_kernel,
        grid_spec=grid_spec,
        out_shape=jax.ShapeDtypeStruct((n_slots, f), BF16),
        compiler_params=_cparams(("arbitrary", "arbitrary")),
        name="moe_up",
    )(tile_expert, n_used, a_packed, w1, w3)


def _moe_down_kernel(te_ref, nused_ref, a_ref, w_ref, o_ref, wb_ref):
    w = pl.program_id(1)
    used = w < nused_ref[0]

    @pl.when(used & _fresh_expert(te_ref, w))
    def _():
        wb_ref[...] = w_ref[...].astype(BF16)

    @pl.when(used)
    def _():
        o_ref[...] = jnp.dot(a_ref[...], wb_ref[...], preferred_element_type=F32)

    @pl.when(jnp.logical_not(used))
    def _():
        o_ref[...] = jnp.zeros_like(o_ref)


def _moe_down(hid, w2, tile_expert, n_used, tn=1024):
    n_slots, k = hid.shape
    n_tiles = n_slots // MOE_TM
    n = w2.shape[2]
    grid_spec = pltpu.PrefetchScalarGridSpec(
        num_scalar_prefetch=2,
        grid=(n // tn, n_tiles),
        in_specs=[pl.BlockSpec((MOE_TM, k), lambda j, w, te, nu: (_tile_idx(w, nu), 0)),
                  pl.BlockSpec((None, k, tn), lambda j, w, te, nu: (te[_tile_idx(w, nu)], 0, j))],
        out_specs=pl.BlockSpec((MOE_TM, tn), lambda j, w, te, nu: (w, j)),
        scratch_shapes=[pltpu.VMEM((k, tn), BF16)],
    )
    return pl.pallas_call(
        _moe_down_kernel,
        grid_spec=grid_spec,
        out_shape=jax.ShapeDtypeStruct((n_slots, n), F32),
        compiler_params=_cparams(("arbitrary", "arbitrary")),
        name="moe_down",
    )(tile_expert, n_used, hid, w2)


COMBINE_ROWS = 128


def _combine_kernel(pos_ref, nxt_ref, h_ref, wts_ref, gate_ref, lg_ref, lb_ref, y_ref, o_ref, buf_ref, sem):
    def bufs(slot):
        return (buf_ref.at[slot, 0], buf_ref.at[slot, 1])

    slot = _double_buffered_rows(y_ref, bufs, sem, pos_ref, nxt_ref, COMBINE_ROWS)
    wts = wts_ref[...]
    f = buf_ref[slot, 0] * wts[:, 0:1] + buf_ref[slot, 1] * wts[:, 1:2]
    o_ref[...] = _residual_norm(h_ref[...], f, gate_ref[...], lg_ref[...], lb_ref[...])


def _moe_combine(geom, h, y_sorted, pos, wts, mods, gate_slot, ln_g, ln_b):
    rows = geom.n_lat
    steps = rows // COMBINE_ROWS
    per_mod = ROW_TILE // COMBINE_ROWS
    row = pl.BlockSpec((COMBINE_ROWS, D_MODEL), lambda i: (i, 0))
    vec = pl.BlockSpec((1, D_MODEL), lambda i: (0, 0))
    cur, nxt = _idx_specs(2 * COMBINE_ROWS, steps)
    pos3 = pos.reshape(steps, 1, 2 * COMBINE_ROWS)
    return pl.pallas_call(
        _combine_kernel,
        grid=(steps,),
        in_specs=[cur, nxt, row,
                  pl.BlockSpec((COMBINE_ROWS, LANES), lambda i: (i, 0)),
                  pl.BlockSpec((None, 1, D_MODEL), lambda i: (geom.mod_row(i // per_mod) * 6 + gate_slot, 0, 0)),
                  vec, vec,
                  pl.BlockSpec(memory_space=pl.ANY)],
        out_specs=row,
        out_shape=jax.ShapeDtypeStruct((rows, D_MODEL), F32),
        scratch_shapes=[pltpu.VMEM((2, 2, COMBINE_ROWS, D_MODEL), F32), pltpu.SemaphoreType.DMA((2,))],
        compiler_params=_cparams(("arbitrary",)),
        name="moe_combine",
    )(pos3, pos3, h, wts, mods, ln_g.reshape(1, D_MODEL), ln_b.reshape(1, D_MODEL), y_sorted)


def _route(top_idx, n_tokens):
    flat = top_idx.reshape(-1)
    onehot = (flat[:, None] == jnp.arange(N_EXPERTS, dtype=jnp.int32)[None, :]).astype(jnp.int32)
    blk = 2 * ROW_TILE
    inner = jnp.cumsum(onehot.reshape(-1, blk, N_EXPERTS), axis=1)
    blk_tot = inner[:, -1, :]
    blk_off = jnp.cumsum(blk_tot, axis=0) - blk_tot
    csum = (inner + blk_off[:, None, :]).reshape(-1, N_EXPERTS)
    rank = jnp.sum(csum * onehot, axis=1) - 1
    counts = csum[-1]
    tiles_per = (counts + MOE_TM - 1) // MOE_TM
    tile_end = jnp.cumsum(tiles_per)
    tile_start = tile_end - tiles_per
    pos = (tile_start[flat] * MOE_TM + rank).astype(jnp.int32)
    n_tiles = (2 * n_tokens) // MOE_TM + N_EXPERTS
    n_used = tile_end[-1].astype(jnp.int32)
    tile_ids = jnp.arange(n_tiles, dtype=jnp.int32)
    tile_expert = jnp.sum((tile_ids[:, None] >= tile_end[None, :]).astype(jnp.int32), axis=1)
    tile_expert = jnp.minimum(tile_expert, N_EXPERTS - 1).astype(jnp.int32)
    slot_token = jnp.zeros((n_tiles * MOE_TM,), jnp.int32).at[pos].set(
        jnp.arange(2 * n_tokens, dtype=jnp.int32) // 2)
    return pos, slot_token, tile_expert, n_used.reshape(1)


def kernel(x, c, ctx, c_ctx, ada_w, ada_b, w_in, mamba_conv_w, mamba_conv_b, dt_bias_fwd, dt_bias_bwd, a_log_fwd, a_log_bwd, d_skip, ssm_norm_w, conf_conv_w, conf_conv_b, conf_ln_g, conf_ln_b, w_out, ln1_g, ln1_b, ln2_g, ln2_b, ffn_w1, ffn_w3, ffn_w2, router_w, router_b, moe_w1, moe_w3, moe_w2):
    bsz, seq, d = x.shape
    ctx_len = ctx.shape[1]
    geom = _Geom(bsz, seq, ctx_len)
    n_all, n_lat = geom.n_all, geom.n_lat

    mod_rows = SUBLANES
    cond_rows = jnp.zeros((mod_rows, d), F32).at[:bsz].set(c).at[bsz].set(c_ctx)
    mods_all = _ada_mods(cond_rows, ada_w, ada_b).reshape(DEPTH, mod_rows * 6, 1, d)

    h, u = _pack_modulate(geom, x.reshape(n_lat, d), ctx.reshape(bsz * ctx_len, d), mods_all[0])

    tm_all = _pick_tile(n_all, 1056)
    tm_lat = _pick_tile(n_lat, 1024)

    for i in range(DEPTH):
        mods = mods_all[i]
        last = i == DEPTH - 1
        w_main = jnp.concatenate([w_in[i][:, :OFF_DT], w_in[i][:, OFF_X:]], axis=1).astype(BF16)
        w_dt = jnp.pad(w_in[i][:, OFF_DT:OFF_X], ((0, 0), (0, LANES - 2 * SSM_HEADS))).astype(BF16)
        w_o = w_out[i].astype(BF16)

        proj = _matmul([(u, w_main)], n_all, tm_all, 1024, F32, "in_proj")
        dtr = _matmul([(u, w_dt)], n_all, tm_all, LANES, F32, "dt_proj")
        dtt = jnp.transpose(dtr[:, :2 * SSM_HEADS])

        xs, bc = _ssm_conv(geom, proj, mamba_conv_w[i], mamba_conv_b[i])
        y_b = _ssd_pass(geom, xs, bc, dtr, dtt, dt_bias_bwd[i], a_log_bwd[i], rev=True)
        dsk = jnp.repeat(d_skip[i], SSM_HEADDIM).reshape(1, D_SSM)
        y_ssm = _ssd_pass(geom, xs, bc, dtr, dtt, dt_bias_fwd[i], a_log_fwd[i], rev=False,
                          final_args=(y_b, proj, dsk, ssm_norm_w[i].reshape(1, D_SSM)))

        if i % 2 == 0:
            v = _conformer_rows(geom, proj, conf_conv_w[i], conf_conv_b[i], conf_ln_g[i], conf_ln_b[i])
        else:
            conv = _conformer_cols_conv(geom, proj, conf_conv_w[i], conf_conv_b[i])
            v = _ln_swish(conv, conf_ln_g[i], conf_ln_b[i])

        rows = n_lat if last else n_all
        tm = tm_lat if last else tm_all
        mix = _matmul([(y_ssm, w_o[:D_SSM]), (v, w_o[D_SSM:])], rows, tm, 1024, F32, "out_proj")

        j = i // 2
        if i % 2 == 0:
            h, u2 = _epilogue(geom, h, mix, rows, mods, 2, ln1_g[i], ln1_b[i], mods, (3, 4))
            w1 = jnp.pad(ffn_w1[j], ((0, 0), (0, FFN_PAD - FFN_DENSE))).astype(BF16)
            w3 = jnp.pad(ffn_w3[j], ((0, 0), (0, FFN_PAD - FFN_DENSE))).astype(BF16)
            w2 = jnp.pad(ffn_w2[j], ((0, FFN_PAD - FFN_DENSE), (0, 0))).astype(BF16)
            hid = _gated_up(u2, w1, w3, rows, tm, 512)
            f = _matmul_ktiled(hid, w2, rows, tm, 2048, 1024, "ffn_down")
            if last:
                h = _epilogue(geom, h, f, rows, mods, 5, ln2_g[i], ln2_b[i])[0]
            else:
                h, u = _epilogue(geom, h, f, rows, mods, 5, ln2_g[i], ln2_b[i], mods_all[i + 1], (0, 1))
        else:
            rw = jnp.pad(router_w[j], ((0, 0), (0, LANES - N_EXPERTS)))
            rw_hi = rw.astype(BF16)
            rw_lo = (rw - rw_hi.astype(F32)).astype(BF16)
            rb = jnp.pad(router_b[j], (0, LANES - N_EXPERTS)).reshape(1, LANES)
            assert last, "a routed layer is only supported as the final layer"
            h, u2p, top_idx, top_w = _epilogue(geom, h, mix, rows, mods, 2, ln1_g[i], ln1_b[i], mods, (3, 4),
                                                  router=(rw_hi, rw_lo, rb))
            pos, slot_token, tile_expert, n_used = _route(top_idx[:, :2], rows)
            a_sorted = _gather_rows(u2p, slot_token)
            hid = _moe_up(a_sorted, moe_w1[j], moe_w3[j], tile_expert, n_used)
            y_sorted = _moe_down(hid, moe_w2[j], tile_expert, n_used)
            h = _moe_combine(geom, h, y_sorted, pos, top_w, mods, 5, ln2_g[i], ln2_b[i])

    return h[:n_lat].reshape(bsz, seq, d)
```

```python
import functools

import jax
import jax.numpy as jnp
from jax import lax
from jax.experimental import pallas as pl
from jax.experimental.pallas import tpu as pltpu

F32 = jnp.float32
BF16 = jnp.bfloat16

D_MODEL = 4096
DEPTH = 2
GRID_W = 64
D_SSM = 2048
D_CONF = 2048
SSM_HEADDIM = 64
SSM_HEADS = 32
SSM_GROUPS = 8
HEADS_PER_GROUP = 4
SSM_STATE = 128
SSM_CONV = 5
CHUNK = 128
GROUP_W = HEADS_PER_GROUP * SSM_HEADDIM
CONF_WIDTH = 31
CONF_HALF = CONF_WIDTH // 2
OFF_DT = D_SSM
OFF_X = OFF_DT + 2 * SSM_HEADS
N_MAIN = 10240
COL_X, COL_BC, COL_VAL, COL_GATE = 1, 2, 3, 4
FFN_DENSE = 11008
FFN_PAD = 11264
N_EXPERTS = 8
FFN_EXPERT = 3584
LN_EPS = 1e-5
ALPHA = (2 * DEPTH) ** 0.25

LANES = 128
SUBLANES = 8
ROW_TILE = 256
MOE_TM = 512
VMEM_LIMIT = 56 * 1024 * 1024


def _cparams(sem, vmem=VMEM_LIMIT):
    return pltpu.CompilerParams(dimension_semantics=sem, vmem_limit_bytes=vmem)


def _sigmoid(x):
    return 1.0 / (1.0 + jnp.exp(-x))


def _silu(x):
    return x * _sigmoid(x)


def _softplus(x):
    return jnp.maximum(x, 0.0) + jnp.log(1.0 + jnp.exp(-jnp.abs(x)))


def _standardize(x):
    mu = jnp.mean(x, axis=-1, keepdims=True)
    xc = x - mu
    var = jnp.mean(xc * xc, axis=-1, keepdims=True)
    return xc * lax.rsqrt(var + LN_EPS)


def _pick_tile(n, target, mult=16):
    best = None
    for t in range(mult, min(n, target) + 1, mult):
        if n % t == 0:
            best = t
    assert best is not None, (n, target)
    return best


def _ada_kernel(c_ref, w_ref, b_ref, o_ref):
    cond = _silu(c_ref[...]).astype(BF16)
    o_ref[...] = jnp.dot(cond, w_ref[...].astype(BF16), preferred_element_type=F32) + b_ref[...]


def _ada_mods(cond_rows, ada_w, ada_b):
    depth, d, n = ada_w.shape
    rows = cond_rows.shape[0]
    tn = 1024
    return pl.pallas_call(
        _ada_kernel,
        grid=(depth, n // tn),
        in_specs=[
            pl.BlockSpec((rows, d), lambda l, j: (0, 0)),
            pl.BlockSpec((None, d, tn), lambda l, j: (l, 0, j)),
            pl.BlockSpec((None, 1, tn), lambda l, j: (l, 0, j)),
        ],
        out_specs=pl.BlockSpec((None, rows, tn), lambda l, j: (l, 0, j)),
        out_shape=jax.ShapeDtypeStruct((depth, rows, n), F32),
        compiler_params=_cparams(("arbitrary", "arbitrary")),
        name="ada_mods",
    )(cond_rows, ada_w, ada_b.reshape(depth, 1, n))


class _Geom:
    def __init__(self, bsz, seq, ctx_len):
        self.bsz, self.seq, self.ctx = bsz, seq, ctx_len
        self.n_lat = bsz * seq
        self.n_all = self.n_lat + bsz * ctx_len
        assert seq % ROW_TILE == 0 and ctx_len % ROW_TILE == 0 and seq % GRID_W == 0
        self.lat_tiles = self.n_lat // ROW_TILE
        self.tiles_per_seq = seq // ROW_TILE
        self.tiles_per_ctx = ctx_len // ROW_TILE

    def mod_row(self, i):
        return jnp.where(i < self.lat_tiles, i // self.tiles_per_seq, self.bsz)

    def seq_edges(self, i):
        lat = i < self.lat_tiles
        j = i - self.lat_tiles
        first = jnp.where(lat, i % self.tiles_per_seq == 0, j % self.tiles_per_ctx == 0)
        last = jnp.where(lat, i % self.tiles_per_seq == self.tiles_per_seq - 1,
                         j % self.tiles_per_ctx == self.tiles_per_ctx - 1)
        return first, last


def _mod_spec(geom, slot):
    return pl.BlockSpec((None, 1, D_MODEL), lambda i: (geom.mod_row(i) * 6 + slot, 0, 0))


def _split_row_specs(lat_tiles):
    return [pl.BlockSpec((ROW_TILE, D_MODEL), lambda i: (jnp.minimum(i, lat_tiles - 1), 0)),
            pl.BlockSpec((ROW_TILE, D_MODEL), lambda i: (jnp.maximum(i - lat_tiles, 0), 0))]


def _pack_mod_kernel(x_ref, c_ref, sh_ref, sc_ref, u_ref, *, lat_tiles):
    v = jnp.where(pl.program_id(0) < lat_tiles, x_ref[...], c_ref[...])
    u_ref[...] = (_standardize(v) * (1.0 + sc_ref[...]) + sh_ref[...]).astype(BF16)


def _pack_modulate(geom, x2, ctx2, mods):
    n_tiles = geom.n_all // ROW_TILE
    return pl.pallas_call(
        functools.partial(_pack_mod_kernel, lat_tiles=geom.lat_tiles),
        grid=(n_tiles,),
        in_specs=_split_row_specs(geom.lat_tiles) + [_mod_spec(geom, 0), _mod_spec(geom, 1)],
        out_specs=pl.BlockSpec((ROW_TILE, D_MODEL), lambda i: (i, 0)),
        out_shape=jax.ShapeDtypeStruct((geom.n_all, D_MODEL), BF16),
        compiler_params=_cparams(("arbitrary",)),
        name="pack_modulate",
    )(x2, ctx2, mods, mods)


def _mm_kernel(*refs, n_pairs):
    o_ref = refs[-1]
    acc = jnp.dot(refs[0][...], refs[1][...], preferred_element_type=F32)
    for p in range(1, n_pairs):
        acc = acc + jnp.dot(refs[2 * p][...], refs[2 * p + 1][...], preferred_element_type=F32)
    o_ref[...] = acc.astype(o_ref.dtype)


def _matmul(pairs, rows, tm, tn, out_dtype, name):
    n = pairs[0][1].shape[1]
    assert rows % tm == 0 and n % tn == 0
    in_specs, args = [], []
    for a, w in pairs:
        k = a.shape[1]
        in_specs += [pl.BlockSpec((tm, k), lambda i, j: (i, 0)), pl.BlockSpec((k, tn), lambda i, j: (0, j))]
        args += [a, w]
    return pl.pallas_call(
        functools.partial(_mm_kernel, n_pairs=len(pairs)),
        grid=(rows // tm, n // tn),
        in_specs=in_specs,
        out_specs=pl.BlockSpec((tm, tn), lambda i, j: (i, j)),
        out_shape=jax.ShapeDtypeStruct((rows, n), out_dtype),
        compiler_params=_cparams(("arbitrary", "arbitrary")),
        name=name,
    )(*args)


def _gated_up_kernel(a_ref, w1_ref, w3_ref, o_ref, *, n_valid):
    a = a_ref[...]
    h1 = jnp.dot(a, w1_ref[...], preferred_element_type=F32)
    h3 = jnp.dot(a, w3_ref[...], preferred_element_type=F32)
    col = pl.program_id(1) * o_ref.shape[1] + lax.broadcasted_iota(jnp.int32, o_ref.shape, 1)
    o_ref[...] = jnp.where(col < n_valid, _silu(h1) * h3, 0.0).astype(o_ref.dtype)


def _gated_up(a, w1, w3, rows, tm, tn, n_out):
    k, n_valid = w1.shape
    n = n_out
    return pl.pallas_call(
        functools.partial(_gated_up_kernel, n_valid=n_valid),
        grid=(rows // tm, n // tn),
        in_specs=[pl.BlockSpec((tm, k), lambda i, j: (i, 0)),
                  pl.BlockSpec((k, tn), lambda i, j: (0, j)),
                  pl.BlockSpec((k, tn), lambda i, j: (0, j))],
        out_specs=pl.BlockSpec((tm, tn), lambda i, j: (i, j)),
        out_shape=jax.ShapeDtypeStruct((rows, n), BF16),
        compiler_params=_cparams(("arbitrary", "arbitrary")),
        name="ffn_up",
    )(a, w1, w3)


def _mm_acc_kernel(a_ref, w_ref, o_ref, acc_ref):
    k = pl.program_id(2)

    @pl.when(k == 0)
    def _():
        acc_ref[...] = jnp.zeros_like(acc_ref)

    acc_ref[...] += jnp.dot(a_ref[...], w_ref[...], preferred_element_type=F32)

    @pl.when(k == pl.num_programs(2) - 1)
    def _():
        o_ref[...] = acc_ref[...]


def _matmul_ktiled(a, w, rows, tm, tn, tk, name):
    kk, n = w.shape
    assert rows % tm == 0 and n % tn == 0 and kk % tk == 0
    return pl.pallas_call(
        _mm_acc_kernel,
        grid=(rows // tm, n // tn, kk // tk),
        in_specs=[pl.BlockSpec((tm, tk), lambda i, j, k: (i, k)),
                  pl.BlockSpec((tk, tn), lambda i, j, k: (k, j))],
        out_specs=pl.BlockSpec((tm, tn), lambda i, j, k: (i, j)),
        out_shape=jax.ShapeDtypeStruct((rows, n), F32),
        scratch_shapes=[pltpu.VMEM((tm, tn), F32)],
        compiler_params=_cparams(("arbitrary", "arbitrary", "arbitrary")),
        name=name,
    )(a, w)


def _conv5_kernel(xm_ref, xp_ref, xn_ref, bm_ref, bp_ref, bn_ref, w_ref, b_ref, xo_ref, bco_ref, scr_ref,
                  *, geom):
    i = pl.program_id(0)
    first, last = geom.seq_edges(i)
    half = D_SSM
    pad = SUBLANES
    for part, (m_ref, p_ref, n_ref) in enumerate(((xm_ref, xp_ref, xn_ref), (bm_ref, bp_ref, bn_ref))):
        lo = part * half
        scr_ref[0:pad, lo:lo + half] = jnp.where(first, 0.0, p_ref[...])
        scr_ref[pad:pad + ROW_TILE, lo:lo + half] = m_ref[...]
        scr_ref[pad + ROW_TILE:2 * pad + ROW_TILE, lo:lo + half] = jnp.where(last, 0.0, n_ref[...])
    rows_blk, lane_blk = 64, 512
    base = pad - SSM_CONV // 2
    for c0 in range(0, 2 * half, lane_blk):
        wc = w_ref[:, c0:c0 + lane_blk]
        bias = b_ref[:, c0:c0 + lane_blk]
        for r0 in range(0, ROW_TILE, rows_blk):
            acc = bias
            for k in range(SSM_CONV):
                acc = acc + scr_ref[base + r0 + k:base + r0 + k + rows_blk, c0:c0 + lane_blk] * wc[k:k + 1, :]
            val = _silu(acc)
            if c0 < half:
                xo_ref[r0:r0 + rows_blk, c0:c0 + lane_blk] = val
            else:
                bco_ref[r0:r0 + rows_blk, c0 - half:c0 - half + lane_blk] = val.astype(BF16)


def _ssm_conv(geom, proj, conv_w, conv_b):
    n_tiles = geom.n_all // ROW_TILE
    sub = ROW_TILE // SUBLANES
    last_blk = geom.n_all // SUBLANES - 1

    def main(col):
        return pl.BlockSpec((ROW_TILE, D_SSM), lambda i: (i, col))

    def prev(col):
        return pl.BlockSpec((SUBLANES, D_SSM), lambda i: (jnp.maximum(i * sub - 1, 0), col))

    def nxt(col):
        return pl.BlockSpec((SUBLANES, D_SSM), lambda i: (jnp.minimum((i + 1) * sub, last_blk), col))

    width = 2 * D_SSM
    return pl.pallas_call(
        functools.partial(_conv5_kernel, geom=geom),
        grid=(n_tiles,),
        in_specs=[main(COL_X), prev(COL_X), nxt(COL_X), main(COL_BC), prev(COL_BC), nxt(COL_BC),
                  pl.BlockSpec((SSM_CONV, width), lambda i: (0, 0)),
                  pl.BlockSpec((1, width), lambda i: (0, 0))],
        out_specs=[pl.BlockSpec((ROW_TILE, D_SSM), lambda i: (i, 0)),
                   pl.BlockSpec((ROW_TILE, D_SSM), lambda i: (i, 0))],
        out_shape=[jax.ShapeDtypeStruct((geom.n_all, D_SSM), F32),
                   jax.ShapeDtypeStruct((geom.n_all, D_SSM), BF16)],
        scratch_shapes=[pltpu.VMEM((ROW_TILE + 2 * SUBLANES, width), F32)],
        compiler_params=_cparams(("arbitrary",)),
        name="ssm_conv",
    )(proj, proj, proj, proj, proj, proj, conv_w, conv_b.reshape(1, width))


def _split3_dot(lhs_fn, v):
    v1 = v.astype(BF16)
    r1 = v - v1.astype(F32)
    v2 = r1.astype(BF16)
    v3 = (r1 - v2.astype(F32)).astype(BF16)
    return lhs_fn(v1) + lhs_fn(v2) + lhs_fn(v3)


def _ssd_kernel(*refs, rev, final):
    if final:
        (x_ref, bc_ref, dtr_ref, dtt_ref, pbr_ref, pbc_ref, alr_ref, alc_ref,
         yb_ref, z_ref, dsk_ref, nw_ref, o_ref, s_ref) = refs
    else:
        x_ref, bc_ref, dtr_ref, dtt_ref, pbr_ref, pbc_ref, alr_ref, alc_ref, o_ref, s_ref = refs
    step = pl.program_id(1)

    @pl.when(step == 0)
    def _():
        s_ref[...] = jnp.zeros_like(s_ref)

    n = CHUNK
    row = lax.broadcasted_iota(jnp.int32, (n, n), 0)
    col = lax.broadcasted_iota(jnp.int32, (n, n), 1)
    tri = (row <= col) if rev else (row >= col)
    cum_l = tri.astype(BF16)
    cum_r = jnp.logical_not(tri) | (row == col)
    cum_r = cum_r.astype(BF16)
    h_off = SSM_HEADS if rev else 0
    edge = 0 if rev else n - 1

    dt = _softplus(dtr_ref[...] + pbr_ref[...])
    a = -dt * jnp.exp(alr_ref[...])
    dtt = _softplus(dtt_ref[...] + pbc_ref[...])
    at = -dtt * jnp.exp(alc_ref[...])
    acum = _split3_dot(lambda v: jnp.dot(cum_l, v, preferred_element_type=F32), a)
    acum_t = _split3_dot(lambda v: jnp.dot(v, cum_r, preferred_element_type=F32), at)

    lane_lo = lax.broadcasted_iota(jnp.int32, (n, LANES), 1) < SSM_HEADDIM
    lane_w = lax.broadcasted_iota(jnp.int32, (n, GROUP_W), 1)

    for g in range(SSM_GROUPS):
        b_g = bc_ref[:, g * SSM_STATE:(g + 1) * SSM_STATE]
        c_g = bc_ref[:, SSM_GROUPS * SSM_STATE + g * SSM_STATE:SSM_GROUPS * SSM_STATE + (g + 1) * SSM_STATE]
        x_g = x_ref[:, g * GROUP_W:(g + 1) * GROUP_W]
        x_gb = x_g.astype(BF16)
        s_g = s_ref[g]
        cb = lax.dot_general(c_g, b_g, (((1,), (1,)), ((), ())), preferred_element_type=F32)

        m_parts, ecols, wcols, etots = [], [], [], []
        for e in range(HEADS_PER_GROUP):
            h = g * HEADS_PER_GROUP + e
            colb = jnp.broadcast_to(acum[:, h_off + h:h_off + h + 1], (n, n))
            rowb = jnp.broadcast_to(acum_t[h:h + 1, :], (n, n))
            dtrow = jnp.broadcast_to(dtt[h:h + 1, :], (n, n))
            dtcol = jnp.broadcast_to(dt[:, h_off + h:h_off + h + 1], (n, n))
            decay = jnp.exp(jnp.where(tri, colb - rowb, -jnp.inf))
            m_parts.append((cb * decay * dtrow).astype(BF16))
            ecol = jnp.exp(colb)
            tot = jnp.broadcast_to(colb[edge:edge + 1, :], (n, n))
            ecols.append(ecol)
            wcols.append(jnp.exp(tot - colb) * dtcol)
            etots.append(ecol[edge:edge + 1, :])

        def per_head(parts):
            return jnp.concatenate([jnp.where(lane_lo[:parts[0].shape[0]], parts[0], parts[1]),
                                    jnp.where(lane_lo[:parts[0].shape[0]], parts[2], parts[3])], axis=1)

        scale_e = per_head(ecols)
        scale_w = per_head(wcols)
        scale_t = per_head(etots)

        y = None
        for pair in range(HEADS_PER_GROUP // 2):
            e0, e1 = 2 * pair, 2 * pair + 1
            lhs = jnp.concatenate([m_parts[e0], m_parts[e1]], axis=1)
            r0 = jnp.where((lane_w // SSM_HEADDIM) == e0, x_gb, jnp.zeros_like(x_gb))
            r1 = jnp.where((lane_w // SSM_HEADDIM) == e1, x_gb, jnp.zeros_like(x_gb))
            part = jnp.dot(lhs, jnp.concatenate([r0, r1], axis=0), preferred_element_type=F32)
            y = part if y is None else y + part
        y = y + jnp.dot(c_g, s_g.astype(BF16), preferred_element_type=F32) * scale_e
        xw = (x_g * scale_w).astype(BF16)
        b_t = jnp.transpose(b_g.astype(F32)).astype(BF16)
        s_ref[g] = s_g * scale_t + jnp.dot(b_t, xw, preferred_element_type=F32)

        cols = slice(g * GROUP_W, (g + 1) * GROUP_W)
        if final:
            y = y + yb_ref[:, cols] + x_g * dsk_ref[:, cols]
            gz = y * _silu(z_ref[:, cols])
            ms = jnp.mean(gz * gz, axis=-1, keepdims=True)
            o_ref[:, cols] = (gz * lax.rsqrt(ms + LN_EPS) * nw_ref[:, cols]).astype(o_ref.dtype)
        else:
            o_ref[:, cols] = y


def _ssd_pass(geom, x, bc, dtr, dtt, dt_bias, a_log, rev, final_args=None):
    ncl = geom.seq // CHUNK
    ncc = geom.ctx // CHUNK
    lat_chunks = geom.n_lat // CHUNK

    def chunk(b, s):
        if rev:
            return jnp.where(s < ncc, lat_chunks + b * ncc + (ncc - 1 - s), b * ncl + (ncl - 1 - (s - ncc)))
        return jnp.where(s < ncc, lat_chunks + b * ncc + s, b * ncl + (s - ncc))

    def rows(width):
        return pl.BlockSpec((CHUNK, width), lambda b, s: (chunk(b, s), 0))

    d_idx = 1 if rev else 0
    small = lambda shape: pl.BlockSpec(shape, lambda b, s: (0, 0))
    in_specs = [rows(D_SSM), rows(D_SSM), rows(LANES),
                pl.BlockSpec((SSM_HEADS, CHUNK), lambda b, s: (d_idx, chunk(b, s))),
                small((1, LANES)), small((SSM_HEADS, 1)), small((1, LANES)), small((SSM_HEADS, 1))]
    lane_pad = (d_idx * SSM_HEADS, LANES - (d_idx + 1) * SSM_HEADS)
    args = [x, bc, dtr, dtt, jnp.pad(dt_bias, lane_pad).reshape(1, LANES), dt_bias.reshape(SSM_HEADS, 1),
            jnp.pad(a_log, lane_pad).reshape(1, LANES), a_log.reshape(SSM_HEADS, 1)]
    final = final_args is not None
    if final:
        y_other, proj, dsk, nw = final_args
        in_specs += [rows(D_SSM), rows(D_SSM), small((1, D_SSM)), small((1, D_SSM))]
        args += [y_other, proj, dsk, nw]
    return pl.pallas_call(
        functools.partial(_ssd_kernel, rev=rev, final=final),
        grid=(geom.bsz, ncc + ncl),
        in_specs=in_specs,
        out_specs=rows(D_SSM),
        out_shape=jax.ShapeDtypeStruct((geom.n_all, D_SSM), BF16 if final else F32),
        scratch_shapes=[pltpu.VMEM((SSM_GROUPS, SSM_STATE, GROUP_W), F32)],
        compiler_params=_cparams(("arbitrary", "arbitrary")),
        name="ssd_fwd" if final else "ssd_bwd",
    )(*args)


CONF_LANE_BLK = 512
CONF_SLOT = 2048


def _shift_matrix(run_len):
    l = jnp.arange(run_len)[:, None, None]
    k = jnp.arange(CONF_WIDTH)[None, :, None]
    j = jnp.arange(run_len)[None, None, :]
    return (j == l + k - CONF_HALF).astype(BF16).reshape(run_len, CONF_WIDTH * run_len)


def _conf_rows_kernel(val_ref, gate_ref, s_run_ref, s_ctx_ref, w_ref, cb_ref, g_ref, b_ref, o_ref, rhs_ref, res_ref,
                      *, lat_tiles):
    i = pl.program_id(0)

    def run_conv(run_len, n_runs, s_ref, slots):
        span = CONF_WIDTH * run_len
        it = 0
        for c0 in range(0, D_CONF, CONF_LANE_BLK):
            lanes = slice(c0, c0 + CONF_LANE_BLK)
            for r in range(n_runs):
                rows = slice(r * run_len, (r + 1) * run_len)
                base = (it % slots) * CONF_SLOT
                it += 1
                glu = (val_ref[rows, lanes] * _sigmoid(gate_ref[rows, lanes])).astype(BF16)
                for k in range(CONF_WIDTH):
                    rhs_ref[base + k * run_len:base + (k + 1) * run_len, :] = (
                        glu * w_ref[k:k + 1, lanes].astype(BF16))
                res_ref[rows, lanes] = (jnp.dot(s_ref[...], rhs_ref[base:base + span, :],
                                                preferred_element_type=F32) + cb_ref[:, lanes])

    @pl.when(i < lat_tiles)
    def _():
        run_conv(GRID_W, ROW_TILE // GRID_W, s_run_ref, 2)

    @pl.when(i >= lat_tiles)
    def _():
        run_conv(ROW_TILE, 1, s_ctx_ref, 1)

    rows_blk = 64
    for r0 in range(0, ROW_TILE, rows_blk):
        v = _standardize(res_ref[r0:r0 + rows_blk, :]) * g_ref[...] + b_ref[...]
        o_ref[r0:r0 + rows_blk, :] = _silu(v).astype(o_ref.dtype)


def _conformer_rows(geom, proj, w, cb, ln_g, ln_b):
    assert geom.ctx == ROW_TILE and CONF_SLOT >= CONF_WIDTH * GRID_W
    n_tiles = geom.n_all // ROW_TILE
    full = lambda shape: pl.BlockSpec(shape, lambda i: (0,) * len(shape))
    return pl.pallas_call(
        functools.partial(_conf_rows_kernel, lat_tiles=geom.lat_tiles),
        grid=(n_tiles,),
        in_specs=[pl.BlockSpec((ROW_TILE, D_CONF), lambda i: (i, COL_VAL)),
                  pl.BlockSpec((ROW_TILE, D_CONF), lambda i: (i, COL_GATE)),
                  full((GRID_W, CONF_WIDTH * GRID_W)), full((ROW_TILE, CONF_WIDTH * ROW_TILE)),
                  full((CONF_WIDTH, D_CONF)), full((1, D_CONF)), full((1, D_CONF)), full((1, D_CONF))],
        out_specs=pl.BlockSpec((ROW_TILE, D_CONF), lambda i: (i, 0)),
        out_shape=jax.ShapeDtypeStruct((geom.n_all, D_CONF), BF16),
        scratch_shapes=[pltpu.VMEM((CONF_WIDTH * ROW_TILE, CONF_LANE_BLK), BF16),
                        pltpu.VMEM((ROW_TILE, D_CONF), F32)],
        compiler_params=_cparams(("arbitrary",)),
        name="conformer_rows",
    )(proj, proj, _shift_matrix(GRID_W), _shift_matrix(ROW_TILE), w, cb.reshape(1, D_CONF),
      ln_g.reshape(1, D_CONF), ln_b.reshape(1, D_CONF))


def _conf_cols_kernel(val_ref, gate_ref, w_ref, cb_ref, o_ref, scr_ref, *, seq):
    halo = CONF_HALF * GRID_W
    scr_ref[0:halo, :] = jnp.zeros((halo, LANES), F32)
    scr_ref[halo + seq:2 * halo + seq, :] = jnp.zeros((halo, LANES), F32)
    blk = 256
    for r0 in range(0, seq, blk):
        scr_ref[halo + r0:halo + r0 + blk, :] = val_ref[r0:r0 + blk, :] * _sigmoid(gate_ref[r0:r0 + blk, :])
    rows_blk = 128

    def body(rb, carry):
        acc = jnp.broadcast_to(cb_ref[...], (rows_blk, LANES))
        for k in range(CONF_WIDTH):
            start = pl.multiple_of(rb * rows_blk + k * GRID_W, GRID_W)
            acc = acc + scr_ref[pl.ds(start, rows_blk), :] * w_ref[k:k + 1, :]
        o_ref[pl.ds(pl.multiple_of(rb * rows_blk, rows_blk), rows_blk), :] = acc
        return carry

    lax.fori_loop(0, seq // rows_blk, body, 0)


def _conformer_cols_conv(geom, proj, w, cb):
    seq = geom.seq
    nblk = D_CONF // LANES
    val0 = COL_VAL * D_CONF // LANES
    gate0 = COL_GATE * D_CONF // LANES
    return pl.pallas_call(
        functools.partial(_conf_cols_kernel, seq=seq),
        grid=(geom.bsz, nblk),
        in_specs=[pl.BlockSpec((seq, LANES), lambda b, c: (b, val0 + c)),
                  pl.BlockSpec((seq, LANES), lambda b, c: (b, gate0 + c)),
                  pl.BlockSpec((CONF_WIDTH, LANES), lambda b, c: (0, c)),
                  pl.BlockSpec((1, LANES), lambda b, c: (0, c))],
        out_specs=pl.BlockSpec((seq, LANES), lambda b, c: (b, c)),
        out_shape=jax.ShapeDtypeStruct((geom.n_lat, D_CONF), F32),
        scratch_shapes=[pltpu.VMEM((seq + 2 * CONF_HALF * GRID_W, LANES), F32)],
        compiler_params=_cparams(("arbitrary", "arbitrary")),
        name="conformer_cols",
    )(proj, proj, w, cb.reshape(1, D_CONF))


def _ln_swish_kernel(v_ref, g_ref, b_ref, o_ref):
    o_ref[...] = _silu(_standardize(v_ref[...]) * g_ref[...] + b_ref[...]).astype(o_ref.dtype)


def _ln_swish(conv, ln_g, ln_b):
    rows = conv.shape[0]
    full = pl.BlockSpec((1, D_CONF), lambda i: (0, 0))
    return pl.pallas_call(
        _ln_swish_kernel,
        grid=(rows // ROW_TILE,),
        in_specs=[pl.BlockSpec((ROW_TILE, D_CONF), lambda i: (i, 0)), full, full],
        out_specs=pl.BlockSpec((ROW_TILE, D_CONF), lambda i: (i, 0)),
        out_shape=jax.ShapeDtypeStruct((rows, D_CONF), BF16),
        compiler_params=_cparams(("arbitrary",)),
        name="conformer_ln",
    )(conv, ln_g.reshape(1, D_CONF), ln_b.reshape(1, D_CONF))


def _top2(logits):
    lane_i = lax.broadcasted_iota(jnp.int32, logits.shape, 1)
    lane = lane_i.astype(F32)
    m1 = jnp.max(logits, axis=-1, keepdims=True)
    i1 = jnp.min(jnp.where(logits == m1, lane, float(LANES)), axis=-1, keepdims=True)
    rest = jnp.where(lane == i1, -jnp.inf, logits)
    m2 = jnp.max(rest, axis=-1, keepdims=True)
    i2 = jnp.min(jnp.where(rest == m2, lane, float(LANES)), axis=-1, keepdims=True)
    e2 = jnp.exp(m2 - m1)
    w1 = 1.0 / (1.0 + e2)
    w2 = e2 / (1.0 + e2)
    idx = jnp.where(lane_i == 0, i1, jnp.where(lane_i == 1, i2, 0.0)).astype(jnp.int32)
    wts = jnp.where(lane_i == 0, w1, jnp.where(lane_i == 1, w2, 0.0))
    return idx, wts


HALF_D = D_MODEL // 2
_HI_MASK = 0xFFFF0000


def _pack_bf16_pairs(u):
    bits = pltpu.bitcast(u.astype(BF16).astype(F32), jnp.uint32)
    return (bits[:, HALF_D:] & jnp.uint32(_HI_MASK)) | (bits[:, :HALF_D] >> 16)


def _unpack_bf16_pairs(p):
    lo = pltpu.bitcast(p << 16, F32).astype(BF16)
    hi = pltpu.bitcast(p & jnp.uint32(_HI_MASK), F32).astype(BF16)
    return lo, hi


def _residual_norm(h, f, gate, ln_g, ln_b):
    return _standardize(ALPHA * h + gate * f) * ln_g + ln_b


def _epilogue_kernel(*refs, modulate, router, lat_tiles):
    if lat_tiles is not None:
        h_in = jnp.where(pl.program_id(0) < lat_tiles, refs[0][...], refs[1][...])
        refs = refs[1:]
    else:
        h_in = refs[0][...]
    _, f_ref, gate_ref, lg_ref, lb_ref = refs[:5]
    pos = 5
    if modulate:
        sh_ref, sc_ref = refs[pos:pos + 2]
        pos += 2
    if router:
        rwh_ref, rwl_ref, rb_ref = refs[pos:pos + 3]
        pos += 3
    outs = refs[pos:]
    hn = _residual_norm(h_in, f_ref[...], gate_ref[...], lg_ref[...], lb_ref[...])
    outs[0][...] = hn
    if modulate:
        u = _standardize(hn) * (1.0 + sc_ref[...]) + sh_ref[...]
        if router:
            outs[1][...] = _pack_bf16_pairs(u)
        else:
            outs[1][...] = u.astype(BF16)
    if router:
        u_hi = u.astype(BF16)
        u_lo = (u - u_hi.astype(F32)).astype(BF16)
        logits = (jnp.dot(u_hi, rwh_ref[...], preferred_element_type=F32)
                  + jnp.dot(u_lo, rwh_ref[...], preferred_element_type=F32)
                  + jnp.dot(u_hi, rwl_ref[...], preferred_element_type=F32)) + rb_ref[...]
        lane = lax.broadcasted_iota(jnp.int32, logits.shape, 1)
        logits = jnp.where(lane < N_EXPERTS, logits, -jnp.inf)
        idx, wts = _top2(logits)
        outs[2][...] = idx
        outs[3][...] = wts


def _epilogue(geom, h, f, rows, mods, gate_slot, ln_g, ln_b, next_mods=None, next_slots=None, router=None):
    n_tiles = rows // ROW_TILE
    row = pl.BlockSpec((ROW_TILE, D_MODEL), lambda i: (i, 0))
    vec = pl.BlockSpec((1, D_MODEL), lambda i: (0, 0))
    lat_tiles = None
    if isinstance(h, tuple):
        lat_tiles = geom.lat_tiles
        h_specs, h_args = _split_row_specs(lat_tiles), list(h)
    else:
        h_specs, h_args = [row], [h]
    in_specs = h_specs + [row, _mod_spec(geom, gate_slot), vec, vec]
    args = h_args + [f, mods, ln_g.reshape(1, D_MODEL), ln_b.reshape(1, D_MODEL)]
    out_specs = [row]
    out_shape = [jax.ShapeDtypeStruct((rows, D_MODEL), F32)]
    modulate = next_mods is not None
    if modulate:
        in_specs += [_mod_spec(geom, next_slots[0]), _mod_spec(geom, next_slots[1])]
        args += [next_mods, next_mods]
        if router is not None:
            out_specs.append(pl.BlockSpec((ROW_TILE, HALF_D), lambda i: (i, 0)))
            out_shape.append(jax.ShapeDtypeStruct((rows, HALF_D), jnp.uint32))
        else:
            out_specs.append(row)
            out_shape.append(jax.ShapeDtypeStruct((rows, D_MODEL), BF16))
    if router is not None:
        rw_hi, rw_lo, rb = router
        small = pl.BlockSpec((D_MODEL, LANES), lambda i: (0, 0))
        in_specs += [small, small, pl.BlockSpec((1, LANES), lambda i: (0, 0))]
        args += [rw_hi, rw_lo, rb]
        lane_blk = pl.BlockSpec((ROW_TILE, LANES), lambda i: (i, 0))
        out_specs += [lane_blk, lane_blk]
        out_shape += [jax.ShapeDtypeStruct((rows, LANES), jnp.int32),
                      jax.ShapeDtypeStruct((rows, LANES), F32)]
    return pl.pallas_call(
        functools.partial(_epilogue_kernel, modulate=modulate, router=router is not None, lat_tiles=lat_tiles),
        grid=(n_tiles,),
        in_specs=in_specs,
        out_specs=out_specs,
        out_shape=out_shape,
        compiler_params=_cparams(("arbitrary",)),
        name="epilogue",
    )(*args)


GATHER_ROWS = 256


def _row_copy(src_hbm, dst_vmem, sem, src_row, dst_row):
    return pltpu.make_async_copy(src_hbm.at[pl.ds(src_row, 1), :], dst_vmem.at[pl.ds(dst_row, 1), :], sem)


def _issue_rows(src_ref, dst_refs, sem, idx_fn, n_rows):
    def body(r, carry):
        for p, dst in enumerate(dst_refs):
            _row_copy(src_ref, dst, sem, idx_fn(len(dst_refs) * r + p), r).start(priority=p)
        return carry

    lax.fori_loop(0, n_rows, body, 0, unroll=8)


def _drain_rows(src_ref, dst_refs, sem, n_rows):
    for dst in dst_refs:
        pltpu.make_async_copy(src_ref.at[pl.ds(0, n_rows), :], dst, sem).wait()


def _double_buffered_rows(src_ref, bufs_of_slot, sem, cur_idx_ref, nxt_idx_ref, n_rows):
    i = pl.program_id(0)
    slot = i % 2

    @pl.when(i == 0)
    def _():
        _issue_rows(src_ref, bufs_of_slot(0), sem.at[0], lambda q: cur_idx_ref[0, 0, q], n_rows)

    @pl.when(i + 1 < pl.num_programs(0))
    def _():
        _issue_rows(src_ref, bufs_of_slot(1 - slot), sem.at[1 - slot], lambda q: nxt_idx_ref[0, 0, q], n_rows)

    _drain_rows(src_ref, bufs_of_slot(slot), sem.at[slot], n_rows)
    return slot


def _idx_specs(per_step, steps):
    cur = pl.BlockSpec((1, 1, per_step), lambda i: (i, 0, 0), memory_space=pltpu.SMEM)
    nxt = pl.BlockSpec((1, 1, per_step), lambda i: (jnp.minimum(i + 1, steps - 1), 0, 0), memory_space=pltpu.SMEM)
    return cur, nxt


def _gather_kernel(idx_ref, nxt_ref, src_ref, o_ref, buf_ref, sem):
    half = GATHER_ROWS // 2

    def bufs(slot):
        return (buf_ref.at[slot, 0], buf_ref.at[slot, 1])

    slot = _double_buffered_rows(src_ref, bufs, sem, idx_ref, nxt_ref, half)
    for p in range(2):
        lo, hi = _unpack_bf16_pairs(buf_ref[slot, p])
        o_ref[p, :, :HALF_D] = lo
        o_ref[p, :, HALF_D:] = hi


def _gather_rows(src, slot_token):
    n_slots = slot_token.shape[0]
    width = src.shape[1]
    steps = n_slots // GATHER_ROWS
    half = GATHER_ROWS // 2
    idx = jnp.transpose(slot_token.reshape(steps, 2, half), (0, 2, 1)).reshape(steps, 1, GATHER_ROWS)
    cur, nxt = _idx_specs(GATHER_ROWS, steps)
    out = pl.pallas_call(
        _gather_kernel,
        grid=(steps,),
        in_specs=[cur, nxt, pl.BlockSpec(memory_space=pl.ANY)],
        out_specs=pl.BlockSpec((None, 2, half, 2 * width), lambda i: (i, 0, 0, 0)),
        out_shape=jax.ShapeDtypeStruct((steps, 2, half, 2 * width), BF16),
        scratch_shapes=[pltpu.VMEM((2, 2, half, width), src.dtype), pltpu.SemaphoreType.DMA((2,))],
        compiler_params=_cparams(("arbitrary",)),
        name="moe_gather",
    )(idx, idx, src)
    return out.reshape(n_slots, 2 * width)


def _tile_idx(w, nused_ref):
    return jnp.minimum(w, nused_ref[0] - 1)


def _fresh_expert(te_ref, w):
    return (w == 0) | (te_ref[w] != te_ref[jnp.maximum(w - 1, 0)])


def _moe_up_kernel(te_ref, nused_ref, a_ref, w1_ref, w3_ref, o_ref, w1b_ref, w3b_ref):
    w = pl.program_id(1)
    used = w < nused_ref[0]

    @pl.when(used & _fresh_expert(te_ref, w))
    def _():
        w1b_ref[...] = w1_ref[...].astype(BF16)
        w3b_ref[...] = w3_ref[...].astype(BF16)

    @pl.when(used)
    def _():
        a = a_ref[...]
        h1 = jnp.dot(a, w1b_ref[...], preferred_element_type=F32)
        h3 = jnp.dot(a, w3b_ref[...], preferred_element_type=F32)
        o_ref[...] = (_silu(h1) * h3).astype(o_ref.dtype)

    @pl.when(jnp.logical_not(used))
    def _():
        o_ref[...] = jnp.zeros_like(o_ref)


MOE_UP_VMEM = 59 * 1024 * 1024


def _moe_up(a_sorted, w1, w3, tile_expert, n_used, tf=512):
    n_slots, k = a_sorted.shape
    n_tiles = n_slots // MOE_TM
    f = w1.shape[2]
    w_map = lambda j, w, te, nu: (te[_tile_idx(w, nu)], 0, j)
    grid_spec = pltpu.PrefetchScalarGridSpec(
        num_scalar_prefetch=2,
        grid=(f // tf, n_tiles),
        in_specs=[pl.BlockSpec((MOE_TM, k), lambda j, w, te, nu: (_tile_idx(w, nu), 0)),
                  pl.BlockSpec((None, k, tf), w_map),
                  pl.BlockSpec((None, k, tf), w_map)],
        out_specs=pl.BlockSpec((MOE_TM, tf), lambda j, w, te, nu: (w, j)),
        scratch_shapes=[pltpu.VMEM((k, tf), BF16), pltpu.VMEM((k, tf), BF16)],
    )
    return pl.pallas_call(
        _moe_up_kernel,
        grid_spec=grid_spec,
        out_shape=jax.ShapeDtypeStruct((n_slots, f), BF16),
        compiler_params=_cparams(("arbitrary", "arbitrary"), MOE_UP_VMEM),
        name="moe_up",
    )(tile_expert, n_used, a_sorted, w1, w3)


def _moe_down_kernel(te_ref, nused_ref, a_ref, w_ref, o_ref, wb_ref):
    w = pl.program_id(1)
    used = w < nused_ref[0]

    @pl.when(used & _fresh_expert(te_ref, w))
    def _():
        wb_ref[...] = w_ref[...].astype(BF16)

    @pl.when(used)
    def _():
        o_ref[...] = jnp.dot(a_ref[...], wb_ref[...], preferred_element_type=F32)

    @pl.when(jnp.logical_not(used))
    def _():
        o_ref[...] = jnp.zeros_like(o_ref)


def _moe_down(hid, w2, tile_expert, n_used, tn=1024):
    n_slots, k = hid.shape
    n_tiles = n_slots // MOE_TM
    n = w2.shape[2]
    grid_spec = pltpu.PrefetchScalarGridSpec(
        num_scalar_prefetch=2,
        grid=(n // tn, n_tiles),
        in_specs=[pl.BlockSpec((MOE_TM, k), lambda j, w, te, nu: (_tile_idx(w, nu), 0)),
                  pl.BlockSpec((None, k, tn), lambda j, w, te, nu: (te[_tile_idx(w, nu)], 0, j))],
        out_specs=pl.BlockSpec((MOE_TM, tn), lambda j, w, te, nu: (w, j)),
        scratch_shapes=[pltpu.VMEM((k, tn), BF16)],
    )
    return pl.pallas_call(
        _moe_down_kernel,
        grid_spec=grid_spec,
        out_shape=jax.ShapeDtypeStruct((n_slots, n), F32),
        compiler_params=_cparams(("arbitrary", "arbitrary")),
        name="moe_down",
    )(tile_expert, n_used, hid, w2)


COMBINE_ROWS = 128


def _combine_kernel(pos_ref, nxt_ref, h_ref, wts_ref, gate_ref, lg_ref, lb_ref, y_ref, o_ref, buf_ref, sem):
    def bufs(slot):
        return (buf_ref.at[slot, 0], buf_ref.at[slot, 1])

    slot = _double_buffered_rows(y_ref, bufs, sem, pos_ref, nxt_ref, COMBINE_ROWS)
    wts = wts_ref[...]
    f = buf_ref[slot, 0] * wts[:, 0:1] + buf_ref[slot, 1] * wts[:, 1:2]
    o_ref[...] = _residual_norm(h_ref[...], f, gate_ref[...], lg_ref[...], lb_ref[...])


def _moe_combine(geom, h, y_sorted, pos, wts, mods, gate_slot, ln_g, ln_b):
    rows = geom.n_lat
    steps = rows // COMBINE_ROWS
    per_mod = ROW_TILE // COMBINE_ROWS
    row = pl.BlockSpec((COMBINE_ROWS, D_MODEL), lambda i: (i, 0))
    vec = pl.BlockSpec((1, D_MODEL), lambda i: (0, 0))
    cur, nxt = _idx_specs(2 * COMBINE_ROWS, steps)
    pos3 = pos.reshape(steps, 1, 2 * COMBINE_ROWS)
    return pl.pallas_call(
        _combine_kernel,
        grid=(steps,),
        in_specs=[cur, nxt, row,
                  pl.BlockSpec((COMBINE_ROWS, LANES), lambda i: (i, 0)),
                  pl.BlockSpec((None, 1, D_MODEL), lambda i: (geom.mod_row(i // per_mod) * 6 + gate_slot, 0, 0)),
                  vec, vec,
                  pl.BlockSpec(memory_space=pl.ANY)],
        out_specs=row,
        out_shape=jax.ShapeDtypeStruct((rows, D_MODEL), F32),
        scratch_shapes=[pltpu.VMEM((2, 2, COMBINE_ROWS, D_MODEL), F32), pltpu.SemaphoreType.DMA((2,))],
        compiler_params=_cparams(("arbitrary",)),
        name="moe_combine",
    )(pos3, pos3, h, wts, mods, ln_g.reshape(1, D_MODEL), ln_b.reshape(1, D_MODEL), y_sorted)


def _route(top_idx, n_tokens):
    flat = top_idx.reshape(-1)
    onehot = (jnp.arange(N_EXPERTS, dtype=jnp.int32)[:, None] == flat[None, :]).astype(jnp.int32)
    blk = 2 * ROW_TILE
    inner = jnp.cumsum(onehot.reshape(N_EXPERTS, -1, blk), axis=2)
    blk_tot = inner[:, :, -1]
    blk_off = jnp.cumsum(blk_tot, axis=1) - blk_tot
    csum = (inner + blk_off[:, :, None]).reshape(N_EXPERTS, -1)
    rank = jnp.sum(csum * onehot, axis=0) - 1
    counts = csum[:, -1]
    tiles_per = (counts + MOE_TM - 1) // MOE_TM
    tile_end = jnp.cumsum(tiles_per)
    tile_start = tile_end - tiles_per
    pos = (jnp.sum(onehot * (tile_start * MOE_TM)[:, None], axis=0) + rank).astype(jnp.int32)
    n_tiles = (2 * n_tokens) // MOE_TM + N_EXPERTS
    n_used = tile_end[-1].astype(jnp.int32)
    tile_ids = jnp.arange(n_tiles, dtype=jnp.int32)
    tile_expert = jnp.sum((tile_ids[:, None] >= tile_end[None, :]).astype(jnp.int32), axis=1)
    tile_expert = jnp.minimum(tile_expert, N_EXPERTS - 1).astype(jnp.int32)
    slot_token = jnp.zeros((n_tiles * MOE_TM,), jnp.int32).at[pos].set(
        jnp.arange(2 * n_tokens, dtype=jnp.int32) // 2)
    return pos, slot_token, tile_expert, n_used.reshape(1)


def kernel(x, c, ctx, c_ctx, ada_w, ada_b, w_in, mamba_conv_w, mamba_conv_b, dt_bias_fwd, dt_bias_bwd, a_log_fwd, a_log_bwd, d_skip, ssm_norm_w, conf_conv_w, conf_conv_b, conf_ln_g, conf_ln_b, w_out, ln1_g, ln1_b, ln2_g, ln2_b, ffn_w1, ffn_w3, ffn_w2, router_w, router_b, moe_w1, moe_w3, moe_w2):
    bsz, seq, d = x.shape
    ctx_len = ctx.shape[1]
    geom = _Geom(bsz, seq, ctx_len)
    n_all, n_lat = geom.n_all, geom.n_lat

    mod_rows = SUBLANES
    cond_rows = jnp.zeros((mod_rows, d), F32).at[:bsz].set(c).at[bsz].set(c_ctx)
    mods_all = _ada_mods(cond_rows, ada_w, ada_b).reshape(DEPTH, mod_rows * 6, 1, d)

    h = (x.reshape(n_lat, d), ctx.reshape(bsz * ctx_len, d))
    u = _pack_modulate(geom, h[0], h[1], mods_all[0])

    tm_all = _pick_tile(n_all, 1056)
    tm_lat = _pick_tile(n_lat, 1024)

    for i in range(DEPTH):
        mods = mods_all[i]
        last = i == DEPTH - 1
        w_main = jnp.concatenate([w_in[i][:, :OFF_DT], w_in[i][:, OFF_X:]], axis=1).astype(BF16)
        w_dt = jnp.pad(w_in[i][:, OFF_DT:OFF_X], ((0, 0), (0, LANES - 2 * SSM_HEADS))).astype(BF16)
        w_o = w_out[i].astype(BF16)

        proj = _matmul([(u, w_main)], n_all, tm_all, 1024, F32, "in_proj")
        dtr = _matmul([(u, w_dt)], n_all, tm_all, LANES, F32, "dt_proj")
        dtt = jnp.transpose(dtr[:, :2 * SSM_HEADS])

        xs, bc = _ssm_conv(geom, proj, mamba_conv_w[i], mamba_conv_b[i])
        y_b = _ssd_pass(geom, xs, bc, dtr, dtt, dt_bias_bwd[i], a_log_bwd[i], rev=True)
        dsk = jnp.repeat(d_skip[i], SSM_HEADDIM).reshape(1, D_SSM)
        y_ssm = _ssd_pass(geom, xs, bc, dtr, dtt, dt_bias_fwd[i], a_log_fwd[i], rev=False,
                          final_args=(y_b, proj, dsk, ssm_norm_w[i].reshape(1, D_SSM)))

        if i % 2 == 0:
            v = _conformer_rows(geom, proj, conf_conv_w[i], conf_conv_b[i], conf_ln_g[i], conf_ln_b[i])
        else:
            conv = _conformer_cols_conv(geom, proj, conf_conv_w[i], conf_conv_b[i])
            v = _ln_swish(conv, conf_ln_g[i], conf_ln_b[i])

        rows = n_lat if last else n_all
        tm = tm_lat if last else tm_all
        mix = _matmul([(y_ssm, w_o[:D_SSM]), (v, w_o[D_SSM:])], rows, tm, 1024, F32, "out_proj")

        j = i // 2
        if i % 2 == 0:
            h, u2 = _epilogue(geom, h, mix, rows, mods, 2, ln1_g[i], ln1_b[i], mods, (3, 4))
            w1 = ffn_w1[j].astype(BF16)
            w3 = ffn_w3[j].astype(BF16)
            w2 = jnp.pad(ffn_w2[j].astype(BF16), ((0, FFN_PAD - FFN_DENSE), (0, 0)))
            hid = _gated_up(u2, w1, w3, rows, tm, 512, FFN_PAD)
            f = _matmul_ktiled(hid, w2, rows, tm, 2048, 1024, "ffn_down")
            if last:
                h = _epilogue(geom, h, f, rows, mods, 5, ln2_g[i], ln2_b[i])[0]
            else:
                h, u = _epilogue(geom, h, f, rows, mods, 5, ln2_g[i], ln2_b[i], mods_all[i + 1], (0, 1))
        else:
            rw = jnp.pad(router_w[j], ((0, 0), (0, LANES - N_EXPERTS)))
            rw_hi = rw.astype(BF16)
            rw_lo = (rw - rw_hi.astype(F32)).astype(BF16)
            rb = jnp.pad(router_b[j], (0, LANES - N_EXPERTS)).reshape(1, LANES)
            assert last, "a routed layer is only supported as the final layer"
            h, u2p, top_idx, top_w = _epilogue(geom, h, mix, rows, mods, 2, ln1_g[i], ln1_b[i], mods, (3, 4),
                                                  router=(rw_hi, rw_lo, rb))
            pos, slot_token, tile_expert, n_used = _route(top_idx[:, :2], rows)
            a_sorted = _gather_rows(u2p, slot_token)
            hid = _moe_up(a_sorted, moe_w1[j], moe_w3[j], tile_expert, n_used)
            y_sorted = _moe_down(hid, moe_w2[j], tile_expert, n_used)
            h = _moe_combine(geom, h, y_sorted, pos, top_w, mods, 5, ln2_g[i], ln2_b[i])

    return h[:n_lat].reshape(bsz, seq, d)
```

```python
import functools

import jax
import jax.numpy as jnp
from jax import lax
from jax.experimental import pallas as pl
from jax.experimental.pallas import tpu as pltpu

F32 = jnp.float32
BF16 = jnp.bfloat16

D_MODEL = 4096
DEPTH = 2
GRID_W = 64
D_SSM = 2048
D_CONF = 2048
SSM_HEADDIM = 64
SSM_HEADS = 32
SSM_GROUPS = 8
HEADS_PER_GROUP = 4
SSM_STATE = 128
SSM_CONV = 5
CHUNK = 128
GROUP_W = HEADS_PER_GROUP * SSM_HEADDIM
CONF_WIDTH = 31
CONF_HALF = CONF_WIDTH // 2
OFF_DT = D_SSM
OFF_X = OFF_DT + 2 * SSM_HEADS
N_MAIN = 10240
COL_X, COL_BC, COL_VAL, COL_GATE = 1, 2, 3, 4
FFN_DENSE = 11008
FFN_PAD = 11264
N_EXPERTS = 8
FFN_EXPERT = 3584
LN_EPS = 1e-5
ALPHA = (2 * DEPTH) ** 0.25

LANES = 128
SUBLANES = 8
ROW_TILE = 256
MOE_TM = 512
VMEM_LIMIT = 56 * 1024 * 1024


def _cparams(sem, vmem=VMEM_LIMIT):
    return pltpu.CompilerParams(dimension_semantics=sem, vmem_limit_bytes=vmem)


def _sigmoid(x):
    return 1.0 / (1.0 + jnp.exp(-x))


def _silu(x):
    return x * _sigmoid(x)


def _softplus(x):
    return jnp.maximum(x, 0.0) + jnp.log(1.0 + jnp.exp(-jnp.abs(x)))


def _standardize(x):
    mu = jnp.mean(x, axis=-1, keepdims=True)
    xc = x - mu
    var = jnp.mean(xc * xc, axis=-1, keepdims=True)
    return xc * lax.rsqrt(var + LN_EPS)


def _pick_tile(n, target, mult=16):
    best = None
    for t in range(mult, min(n, target) + 1, mult):
        if n % t == 0:
            best = t
    assert best is not None, (n, target)
    return best


def _ada_kernel(c_ref, w_ref, b_ref, o_ref):
    cond = _silu(c_ref[...]).astype(BF16)
    o_ref[...] = jnp.dot(cond, w_ref[...].astype(BF16), preferred_element_type=F32) + b_ref[...]


def _ada_mods(cond_rows, ada_w, ada_b):
    depth, d, n = ada_w.shape
    rows = cond_rows.shape[0]
    tn = 1024
    return pl.pallas_call(
        _ada_kernel,
        grid=(depth, n // tn),
        in_specs=[
            pl.BlockSpec((rows, d), lambda l, j: (0, 0)),
            pl.BlockSpec((None, d, tn), lambda l, j: (l, 0, j)),
            pl.BlockSpec((None, 1, tn), lambda l, j: (l, 0, j)),
        ],
        out_specs=pl.BlockSpec((None, rows, tn), lambda l, j: (l, 0, j)),
        out_shape=jax.ShapeDtypeStruct((depth, rows, n), F32),
        compiler_params=_cparams(("arbitrary", "arbitrary")),
        name="ada_mods",
    )(cond_rows, ada_w, ada_b.reshape(depth, 1, n))


class _Geom:
    def __init__(self, bsz, seq, ctx_len):
        self.bsz, self.seq, self.ctx = bsz, seq, ctx_len
        self.n_lat = bsz * seq
        self.n_all = self.n_lat + bsz * ctx_len
        assert seq % ROW_TILE == 0 and ctx_len % ROW_TILE == 0 and seq % GRID_W == 0
        self.lat_tiles = self.n_lat // ROW_TILE
        self.tiles_per_seq = seq // ROW_TILE
        self.tiles_per_ctx = ctx_len // ROW_TILE

    def mod_row(self, i):
        return jnp.where(i < self.lat_tiles, i // self.tiles_per_seq, self.bsz)

    def seq_edges(self, i):
        lat = i < self.lat_tiles
        j = i - self.lat_tiles
        first = jnp.where(lat, i % self.tiles_per_seq == 0, j % self.tiles_per_ctx == 0)
        last = jnp.where(lat, i % self.tiles_per_seq == self.tiles_per_seq - 1,
                         j % self.tiles_per_ctx == self.tiles_per_ctx - 1)
        return first, last


def _mod_spec(geom, slot):
    return pl.BlockSpec((None, 1, D_MODEL), lambda i: (geom.mod_row(i) * 6 + slot, 0, 0))


def _split_row_specs(lat_tiles):
    return [pl.BlockSpec((ROW_TILE, D_MODEL), lambda i: (jnp.minimum(i, lat_tiles - 1), 0)),
            pl.BlockSpec((ROW_TILE, D_MODEL), lambda i: (jnp.maximum(i - lat_tiles, 0), 0))]


def _pack_mod_kernel(x_ref, c_ref, sh_ref, sc_ref, u_ref, *, lat_tiles):
    v = jnp.where(pl.program_id(0) < lat_tiles, x_ref[...], c_ref[...])
    u_ref[...] = (_standardize(v) * (1.0 + sc_ref[...]) + sh_ref[...]).astype(BF16)


def _pack_modulate(geom, x2, ctx2, mods):
    n_tiles = geom.n_all // ROW_TILE
    return pl.pallas_call(
        functools.partial(_pack_mod_kernel, lat_tiles=geom.lat_tiles),
        grid=(n_tiles,),
        in_specs=_split_row_specs(geom.lat_tiles) + [_mod_spec(geom, 0), _mod_spec(geom, 1)],
        out_specs=pl.BlockSpec((ROW_TILE, D_MODEL), lambda i: (i, 0)),
        out_shape=jax.ShapeDtypeStruct((geom.n_all, D_MODEL), BF16),
        compiler_params=_cparams(("arbitrary",)),
        name="pack_modulate",
    )(x2, ctx2, mods, mods)


def _mm_kernel(*refs, n_pairs):
    o_ref = refs[-1]
    acc = jnp.dot(refs[0][...], refs[1][...], preferred_element_type=F32)
    for p in range(1, n_pairs):
        acc = acc + jnp.dot(refs[2 * p][...], refs[2 * p + 1][...], preferred_element_type=F32)
    o_ref[...] = acc.astype(o_ref.dtype)


def _matmul(pairs, rows, tm, tn, out_dtype, name):
    n = pairs[0][1].shape[1]
    assert rows % tm == 0 and n % tn == 0
    in_specs, args = [], []
    for a, w, row_blk in pairs:
        k = a.shape[1]
        in_specs += [pl.BlockSpec((tm, k), lambda i, j: (i, 0)),
                     pl.BlockSpec((k, tn), lambda i, j, row_blk=row_blk: (row_blk, j))]
        args += [a, w]
    return pl.pallas_call(
        functools.partial(_mm_kernel, n_pairs=len(pairs)),
        grid=(rows // tm, n // tn),
        in_specs=in_specs,
        out_specs=pl.BlockSpec((tm, tn), lambda i, j: (i, j)),
        out_shape=jax.ShapeDtypeStruct((rows, n), out_dtype),
        compiler_params=_cparams(("arbitrary", "arbitrary")),
        name=name,
    )(*args)


def _gated_up_kernel(a_ref, w1_ref, w3_ref, o_ref, *, valid_tiles):
    a = a_ref[...]
    h1 = jnp.dot(a, w1_ref[...].astype(BF16), preferred_element_type=F32)
    h3 = jnp.dot(a, w3_ref[...].astype(BF16), preferred_element_type=F32)
    keep = pl.program_id(1) < valid_tiles
    o_ref[...] = jnp.where(keep, _silu(h1) * h3, 0.0).astype(o_ref.dtype)


def _gated_up(a, w1, w3, rows, tm, tn, n_out):
    k, n_valid = w1.shape
    n = n_out
    assert n_valid % tn == 0 and n % tn == 0
    valid_tiles = n_valid // tn
    w_spec = pl.BlockSpec((k, tn), lambda i, j: (0, jnp.minimum(j, valid_tiles - 1)))
    return pl.pallas_call(
        functools.partial(_gated_up_kernel, valid_tiles=valid_tiles),
        grid=(rows // tm, n // tn),
        in_specs=[pl.BlockSpec((tm, k), lambda i, j: (i, 0)), w_spec, w_spec],
        out_specs=pl.BlockSpec((tm, tn), lambda i, j: (i, j)),
        out_shape=jax.ShapeDtypeStruct((rows, n), BF16),
        compiler_params=_cparams(("arbitrary", "arbitrary")),
        name="ffn_up",
    )(a, w1, w3)


def _mm_acc_kernel(a_ref, w_ref, o_ref, acc_ref):
    k = pl.program_id(2)

    @pl.when(k == 0)
    def _():
        acc_ref[...] = jnp.zeros_like(acc_ref)

    acc_ref[...] += jnp.dot(a_ref[...], w_ref[...], preferred_element_type=F32)

    @pl.when(k == pl.num_programs(2) - 1)
    def _():
        o_ref[...] = acc_ref[...]


def _matmul_ktiled(a, w, rows, tm, tn, tk, name):
    kk, n = w.shape
    assert rows % tm == 0 and n % tn == 0 and kk % tk == 0
    return pl.pallas_call(
        _mm_acc_kernel,
        grid=(rows // tm, n // tn, kk // tk),
        in_specs=[pl.BlockSpec((tm, tk), lambda i, j, k: (i, k)),
                  pl.BlockSpec((tk, tn), lambda i, j, k: (k, j))],
        out_specs=pl.BlockSpec((tm, tn), lambda i, j, k: (i, j)),
        out_shape=jax.ShapeDtypeStruct((rows, n), F32),
        scratch_shapes=[pltpu.VMEM((tm, tn), F32)],
        compiler_params=_cparams(("arbitrary", "arbitrary", "arbitrary")),
        name=name,
    )(a, w)


def _conv5_kernel(xm_ref, xp_ref, xn_ref, bm_ref, bp_ref, bn_ref, w_ref, b_ref, xo_ref, bco_ref, scr_ref,
                  *, geom):
    i = pl.program_id(0)
    first, last = geom.seq_edges(i)
    half = D_SSM
    pad = SUBLANES
    for part, (m_ref, p_ref, n_ref) in enumerate(((xm_ref, xp_ref, xn_ref), (bm_ref, bp_ref, bn_ref))):
        lo = part * half
        scr_ref[0:pad, lo:lo + half] = jnp.where(first, 0.0, p_ref[...])
        scr_ref[pad:pad + ROW_TILE, lo:lo + half] = m_ref[...]
        scr_ref[pad + ROW_TILE:2 * pad + ROW_TILE, lo:lo + half] = jnp.where(last, 0.0, n_ref[...])
    rows_blk, lane_blk = 64, 512
    base = pad - SSM_CONV // 2
    for c0 in range(0, 2 * half, lane_blk):
        wc = w_ref[:, c0:c0 + lane_blk]
        bias = b_ref[:, c0:c0 + lane_blk]
        for r0 in range(0, ROW_TILE, rows_blk):
            acc = bias
            for k in range(SSM_CONV):
                acc = acc + scr_ref[base + r0 + k:base + r0 + k + rows_blk, c0:c0 + lane_blk] * wc[k:k + 1, :]
            val = _silu(acc)
            if c0 < half:
                xo_ref[r0:r0 + rows_blk, c0:c0 + lane_blk] = val
            else:
                bco_ref[r0:r0 + rows_blk, c0 - half:c0 - half + lane_blk] = val.astype(BF16)


def _ssm_conv(geom, proj, conv_w, conv_b):
    n_tiles = geom.n_all // ROW_TILE
    sub = ROW_TILE // SUBLANES
    last_blk = geom.n_all // SUBLANES - 1

    def main(col):
        return pl.BlockSpec((ROW_TILE, D_SSM), lambda i: (i, col))

    def prev(col):
        return pl.BlockSpec((SUBLANES, D_SSM), lambda i: (jnp.maximum(i * sub - 1, 0), col))

    def nxt(col):
        return pl.BlockSpec((SUBLANES, D_SSM), lambda i: (jnp.minimum((i + 1) * sub, last_blk), col))

    width = 2 * D_SSM
    return pl.pallas_call(
        functools.partial(_conv5_kernel, geom=geom),
        grid=(n_tiles,),
        in_specs=[main(COL_X), prev(COL_X), nxt(COL_X), main(COL_BC), prev(COL_BC), nxt(COL_BC),
                  pl.BlockSpec((SSM_CONV, width), lambda i: (0, 0)),
                  pl.BlockSpec((1, width), lambda i: (0, 0))],
        out_specs=[pl.BlockSpec((ROW_TILE, D_SSM), lambda i: (i, 0)),
                   pl.BlockSpec((ROW_TILE, D_SSM), lambda i: (i, 0))],
        out_shape=[jax.ShapeDtypeStruct((geom.n_all, D_SSM), F32),
                   jax.ShapeDtypeStruct((geom.n_all, D_SSM), BF16)],
        scratch_shapes=[pltpu.VMEM((ROW_TILE + 2 * SUBLANES, width), F32)],
        compiler_params=_cparams(("arbitrary",)),
        name="ssm_conv",
    )(proj, proj, proj, proj, proj, proj, conv_w, conv_b.reshape(1, width))


def _split3_dot(lhs_fn, v):
    v1 = v.astype(BF16)
    r1 = v - v1.astype(F32)
    v2 = r1.astype(BF16)
    v3 = (r1 - v2.astype(F32)).astype(BF16)
    return lhs_fn(v1) + lhs_fn(v2) + lhs_fn(v3)


def _ssd_kernel(*refs, rev, final):
    if final:
        (x_ref, bc_ref, dtr_ref, dtt_ref, pbr_ref, pbc_ref, alr_ref, alc_ref,
         yb_ref, z_ref, dsk_ref, nw_ref, o_ref, s_ref) = refs
    else:
        x_ref, bc_ref, dtr_ref, dtt_ref, pbr_ref, pbc_ref, alr_ref, alc_ref, o_ref, s_ref = refs
    step = pl.program_id(1)

    @pl.when(step == 0)
    def _():
        s_ref[...] = jnp.zeros_like(s_ref)

    n = CHUNK
    row = lax.broadcasted_iota(jnp.int32, (n, n), 0)
    col = lax.broadcasted_iota(jnp.int32, (n, n), 1)
    tri = (row <= col) if rev else (row >= col)
    cum_l = tri.astype(BF16)
    cum_r = jnp.logical_not(tri) | (row == col)
    cum_r = cum_r.astype(BF16)
    h_off = SSM_HEADS if rev else 0
    edge = 0 if rev else n - 1

    dt = _softplus(dtr_ref[...] + pbr_ref[...])
    a = -dt * jnp.exp(alr_ref[...])
    dtt = _softplus(dtt_ref[...] + pbc_ref[...])
    at = -dtt * jnp.exp(alc_ref[...])
    acum = _split3_dot(lambda v: jnp.dot(cum_l, v, preferred_element_type=F32), a)
    acum_t = _split3_dot(lambda v: jnp.dot(v, cum_r, preferred_element_type=F32), at)

    lane_lo = lax.broadcasted_iota(jnp.int32, (n, LANES), 1) < SSM_HEADDIM
    lane_w = lax.broadcasted_iota(jnp.int32, (n, GROUP_W), 1)

    for g in range(SSM_GROUPS):
        b_g = bc_ref[:, g * SSM_STATE:(g + 1) * SSM_STATE]
        c_g = bc_ref[:, SSM_GROUPS * SSM_STATE + g * SSM_STATE:SSM_GROUPS * SSM_STATE + (g + 1) * SSM_STATE]
        x_g = x_ref[:, g * GROUP_W:(g + 1) * GROUP_W]
        x_gb = x_g.astype(BF16)
        s_g = s_ref[g]
        cb = lax.dot_general(c_g, b_g, (((1,), (1,)), ((), ())), preferred_element_type=F32)

        m_parts, colbs, dtcols = [], [], []
        for e in range(HEADS_PER_GROUP):
            h = g * HEADS_PER_GROUP + e
            colb = jnp.broadcast_to(acum[:, h_off + h:h_off + h + 1], (n, n))
            rowb = jnp.broadcast_to(acum_t[h:h + 1, :], (n, n))
            dtrow = jnp.broadcast_to(dtt[h:h + 1, :], (n, n))
            decay = jnp.exp(jnp.where(tri, colb - rowb, -jnp.inf))
            m_parts.append((cb * decay * dtrow).astype(BF16))
            colbs.append(colb)
            dtcols.append(jnp.broadcast_to(dt[:, h_off + h:h_off + h + 1], (n, n)))

        ecols, wcols, etots = [], [], []
        for pair in range(HEADS_PER_GROUP // 2):
            colb = jnp.where(lane_lo, colbs[2 * pair], colbs[2 * pair + 1])
            dtcol = jnp.where(lane_lo, dtcols[2 * pair], dtcols[2 * pair + 1])
            ecol = jnp.exp(colb)
            tot = jnp.broadcast_to(colb[edge:edge + 1, :], (n, n))
            ecols.append(ecol)
            wcols.append(jnp.exp(tot - colb) * dtcol)
            etots.append(ecol[edge:edge + 1, :])
        scale_e = jnp.concatenate(ecols, axis=1)
        scale_w = jnp.concatenate(wcols, axis=1)
        scale_t = jnp.concatenate(etots, axis=1)

        y = None
        for pair in range(HEADS_PER_GROUP // 2):
            e0, e1 = 2 * pair, 2 * pair + 1
            lhs = jnp.concatenate([m_parts[e0], m_parts[e1]], axis=1)
            r0 = jnp.where((lane_w // SSM_HEADDIM) == e0, x_gb, jnp.zeros_like(x_gb))
            r1 = jnp.where((lane_w // SSM_HEADDIM) == e1, x_gb, jnp.zeros_like(x_gb))
            part = jnp.dot(lhs, jnp.concatenate([r0, r1], axis=0), preferred_element_type=F32)
            y = part if y is None else y + part
        y = y + jnp.dot(c_g, s_g.astype(BF16), preferred_element_type=F32) * scale_e
        xw = (x_g * scale_w).astype(BF16)
        b_t = jnp.transpose(b_g.astype(F32)).astype(BF16)
        s_ref[g] = s_g * scale_t + jnp.dot(b_t, xw, preferred_element_type=F32)

        cols = slice(g * GROUP_W, (g + 1) * GROUP_W)
        if final:
            y = y + yb_ref[:, cols] + x_g * dsk_ref[:, cols]
            gz = y * _silu(z_ref[:, cols])
            ms = jnp.mean(gz * gz, axis=-1, keepdims=True)
            o_ref[:, cols] = (gz * lax.rsqrt(ms + LN_EPS) * nw_ref[:, cols]).astype(o_ref.dtype)
        else:
            o_ref[:, cols] = y


def _ssd_pass(geom, x, bc, dtr, dtt, dt_bias, a_log, rev, final_args=None):
    ncl = geom.seq // CHUNK
    ncc = geom.ctx // CHUNK
    lat_chunks = geom.n_lat // CHUNK

    def chunk(b, s):
        if rev:
            return jnp.where(s < ncc, lat_chunks + b * ncc + (ncc - 1 - s), b * ncl + (ncl - 1 - (s - ncc)))
        return jnp.where(s < ncc, lat_chunks + b * ncc + s, b * ncl + (s - ncc))

    def rows(width):
        return pl.BlockSpec((CHUNK, width), lambda b, s: (chunk(b, s), 0))

    d_idx = 1 if rev else 0
    small = lambda shape: pl.BlockSpec(shape, lambda b, s: (0, 0))
    in_specs = [rows(D_SSM), rows(D_SSM), rows(LANES),
                pl.BlockSpec((SSM_HEADS, CHUNK), lambda b, s: (d_idx, chunk(b, s))),
                small((1, LANES)), small((SSM_HEADS, 1)), small((1, LANES)), small((SSM_HEADS, 1))]
    lane_pad = (d_idx * SSM_HEADS, LANES - (d_idx + 1) * SSM_HEADS)
    args = [x, bc, dtr, dtt, jnp.pad(dt_bias, lane_pad).reshape(1, LANES), dt_bias.reshape(SSM_HEADS, 1),
            jnp.pad(a_log, lane_pad).reshape(1, LANES), a_log.reshape(SSM_HEADS, 1)]
    final = final_args is not None
    if final:
        y_other, proj, dsk, nw = final_args
        in_specs += [rows(D_SSM), rows(D_SSM), small((1, D_SSM)), small((1, D_SSM))]
        args += [y_other, proj, dsk, nw]
    return pl.pallas_call(
        functools.partial(_ssd_kernel, rev=rev, final=final),
        grid=(geom.bsz, ncc + ncl),
        in_specs=in_specs,
        out_specs=rows(D_SSM),
        out_shape=jax.ShapeDtypeStruct((geom.n_all, D_SSM), BF16 if final else F32),
        scratch_shapes=[pltpu.VMEM((SSM_GROUPS, SSM_STATE, GROUP_W), F32)],
        compiler_params=_cparams(("arbitrary", "arbitrary")),
        name="ssd_fwd" if final else "ssd_bwd",
    )(*args)


CONF_LANE_BLK = 512
CONF_SLOT = 2048


def _shift_matrix(run_len):
    l = jnp.arange(run_len)[:, None, None]
    k = jnp.arange(CONF_WIDTH)[None, :, None]
    j = jnp.arange(run_len)[None, None, :]
    return (j == l + k - CONF_HALF).astype(BF16).reshape(run_len, CONF_WIDTH * run_len)


def _conf_rows_kernel(val_ref, gate_ref, s_run_ref, s_ctx_ref, w_ref, cb_ref, g_ref, b_ref, o_ref, rhs_ref, res_ref,
                      *, lat_tiles):
    i = pl.program_id(0)

    def run_conv(run_len, n_runs, s_ref, slots):
        span = CONF_WIDTH * run_len
        it = 0
        for c0 in range(0, D_CONF, CONF_LANE_BLK):
            lanes = slice(c0, c0 + CONF_LANE_BLK)
            for r in range(n_runs):
                rows = slice(r * run_len, (r + 1) * run_len)
                base = (it % slots) * CONF_SLOT
                it += 1
                glu = (val_ref[rows, lanes] * _sigmoid(gate_ref[rows, lanes])).astype(BF16)
                for k in range(CONF_WIDTH):
                    rhs_ref[base + k * run_len:base + (k + 1) * run_len, :] = (
                        glu * w_ref[k:k + 1, lanes].astype(BF16))
                res_ref[rows, lanes] = (jnp.dot(s_ref[...], rhs_ref[base:base + span, :],
                                                preferred_element_type=F32) + cb_ref[:, lanes])

    @pl.when(i < lat_tiles)
    def _():
        run_conv(GRID_W, ROW_TILE // GRID_W, s_run_ref, 2)

    @pl.when(i >= lat_tiles)
    def _():
        run_conv(ROW_TILE, 1, s_ctx_ref, 1)

    rows_blk = 64
    for r0 in range(0, ROW_TILE, rows_blk):
        v = _standardize(res_ref[r0:r0 + rows_blk, :]) * g_ref[...] + b_ref[...]
        o_ref[r0:r0 + rows_blk, :] = _silu(v).astype(o_ref.dtype)


def _conformer_rows(geom, proj, w, cb, ln_g, ln_b):
    assert geom.ctx == ROW_TILE and CONF_SLOT >= CONF_WIDTH * GRID_W
    n_tiles = geom.n_all // ROW_TILE
    full = lambda shape: pl.BlockSpec(shape, lambda i: (0,) * len(shape))
    return pl.pallas_call(
        functools.partial(_conf_rows_kernel, lat_tiles=geom.lat_tiles),
        grid=(n_tiles,),
        in_specs=[pl.BlockSpec((ROW_TILE, D_CONF), lambda i: (i, COL_VAL)),
                  pl.BlockSpec((ROW_TILE, D_CONF), lambda i: (i, COL_GATE)),
                  full((GRID_W, CONF_WIDTH * GRID_W)), full((ROW_TILE, CONF_WIDTH * ROW_TILE)),
                  full((CONF_WIDTH, D_CONF)), full((1, D_CONF)), full((1, D_CONF)), full((1, D_CONF))],
        out_specs=pl.BlockSpec((ROW_TILE, D_CONF), lambda i: (i, 0)),
        out_shape=jax.ShapeDtypeStruct((geom.n_all, D_CONF), BF16),
        scratch_shapes=[pltpu.VMEM((CONF_WIDTH * ROW_TILE, CONF_LANE_BLK), BF16),
                        pltpu.VMEM((ROW_TILE, D_CONF), F32)],
        compiler_params=_cparams(("arbitrary",)),
        name="conformer_rows",
    )(proj, proj, _shift_matrix(GRID_W), _shift_matrix(ROW_TILE), w, cb.reshape(1, D_CONF),
      ln_g.reshape(1, D_CONF), ln_b.reshape(1, D_CONF))


def _conf_cols_kernel(val_ref, gate_ref, w_ref, cb_ref, o_ref, scr_ref, *, seq):
    halo = CONF_HALF * GRID_W
    scr_ref[0:halo, :] = jnp.zeros((halo, LANES), F32)
    scr_ref[halo + seq:2 * halo + seq, :] = jnp.zeros((halo, LANES), F32)
    blk = 256
    for r0 in range(0, seq, blk):
        scr_ref[halo + r0:halo + r0 + blk, :] = val_ref[r0:r0 + blk, :] * _sigmoid(gate_ref[r0:r0 + blk, :])
    rows_blk = 128

    def body(rb, carry):
        acc = jnp.broadcast_to(cb_ref[...], (rows_blk, LANES))
        for k in range(CONF_WIDTH):
            start = pl.multiple_of(rb * rows_blk + k * GRID_W, GRID_W)
            acc = acc + scr_ref[pl.ds(start, rows_blk), :] * w_ref[k:k + 1, :]
        o_ref[pl.ds(pl.multiple_of(rb * rows_blk, rows_blk), rows_blk), :] = acc
        return carry

    lax.fori_loop(0, seq // rows_blk, body, 0)


def _conformer_cols_conv(geom, proj, w, cb):
    seq = geom.seq
    nblk = D_CONF // LANES
    val0 = COL_VAL * D_CONF // LANES
    gate0 = COL_GATE * D_CONF // LANES
    return pl.pallas_call(
        functools.partial(_conf_cols_kernel, seq=seq),
        grid=(geom.bsz, nblk),
        in_specs=[pl.BlockSpec((seq, LANES), lambda b, c: (b, val0 + c)),
                  pl.BlockSpec((seq, LANES), lambda b, c: (b, gate0 + c)),
                  pl.BlockSpec((CONF_WIDTH, LANES), lambda b, c: (0, c)),
                  pl.BlockSpec((1, LANES), lambda b, c: (0, c))],
        out_specs=pl.BlockSpec((seq, LANES), lambda b, c: (b, c)),
        out_shape=jax.ShapeDtypeStruct((geom.n_lat, D_CONF), F32),
        scratch_shapes=[pltpu.VMEM((seq + 2 * CONF_HALF * GRID_W, LANES), F32)],
        compiler_params=_cparams(("arbitrary", "arbitrary")),
        name="conformer_cols",
    )(proj, proj, w, cb.reshape(1, D_CONF))


def _ln_swish_kernel(v_ref, g_ref, b_ref, o_ref):
    o_ref[...] = _silu(_standardize(v_ref[...]) * g_ref[...] + b_ref[...]).astype(o_ref.dtype)


def _ln_swish(conv, ln_g, ln_b):
    rows = conv.shape[0]
    full = pl.BlockSpec((1, D_CONF), lambda i: (0, 0))
    return pl.pallas_call(
        _ln_swish_kernel,
        grid=(rows // ROW_TILE,),
        in_specs=[pl.BlockSpec((ROW_TILE, D_CONF), lambda i: (i, 0)), full, full],
        out_specs=pl.BlockSpec((ROW_TILE, D_CONF), lambda i: (i, 0)),
        out_shape=jax.ShapeDtypeStruct((rows, D_CONF), BF16),
        compiler_params=_cparams(("arbitrary",)),
        name="conformer_ln",
    )(conv, ln_g.reshape(1, D_CONF), ln_b.reshape(1, D_CONF))


def _top2(logits):
    lane_i = lax.broadcasted_iota(jnp.int32, logits.shape, 1)
    lane = lane_i.astype(F32)
    m1 = jnp.max(logits, axis=-1, keepdims=True)
    i1 = jnp.min(jnp.where(logits == m1, lane, float(LANES)), axis=-1, keepdims=True)
    rest = jnp.where(lane == i1, -jnp.inf, logits)
    m2 = jnp.max(rest, axis=-1, keepdims=True)
    i2 = jnp.min(jnp.where(rest == m2, lane, float(LANES)), axis=-1, keepdims=True)
    e2 = jnp.exp(m2 - m1)
    w1 = 1.0 / (1.0 + e2)
    w2 = e2 / (1.0 + e2)
    idx = jnp.where(lane_i == 0, i1, jnp.where(lane_i == 1, i2, 0.0)).astype(jnp.int32)
    wts = jnp.where(lane_i == 0, w1, jnp.where(lane_i == 1, w2, 0.0))
    return idx, wts


HALF_D = D_MODEL // 2
_HI_MASK = 0xFFFF0000


def _pack_bf16_pairs(u):
    bits = pltpu.bitcast(u.astype(BF16).astype(F32), jnp.uint32)
    return (bits[:, HALF_D:] & jnp.uint32(_HI_MASK)) | (bits[:, :HALF_D] >> 16)


def _unpack_bf16_pairs(p):
    lo = pltpu.bitcast(p << 16, F32).astype(BF16)
    hi = pltpu.bitcast(p & jnp.uint32(_HI_MASK), F32).astype(BF16)
    return lo, hi


def _residual_norm(h, f, gate, ln_g, ln_b):
    return _standardize(ALPHA * h + gate * f) * ln_g + ln_b


def _epilogue_kernel(*refs, modulate, router, lat_tiles):
    if lat_tiles is not None:
        h_in = jnp.where(pl.program_id(0) < lat_tiles, refs[0][...], refs[1][...])
        refs = refs[1:]
    else:
        h_in = refs[0][...]
    _, f_ref, gate_ref, lg_ref, lb_ref = refs[:5]
    pos = 5
    if modulate:
        sh_ref, sc_ref = refs[pos:pos + 2]
        pos += 2
    if router:
        rwh_ref, rwl_ref, rb_ref = refs[pos:pos + 3]
        pos += 3
    outs = refs[pos:]
    hn = _residual_norm(h_in, f_ref[...], gate_ref[...], lg_ref[...], lb_ref[...])
    outs[0][...] = hn
    if modulate:
        u = _standardize(hn) * (1.0 + sc_ref[...]) + sh_ref[...]
        if router:
            outs[1][...] = _pack_bf16_pairs(u)
        else:
            outs[1][...] = u.astype(BF16)
    if router:
        u_hi = u.astype(BF16)
        u_lo = (u - u_hi.astype(F32)).astype(BF16)
        logits = (jnp.dot(u_hi, rwh_ref[...], preferred_element_type=F32)
                  + jnp.dot(u_lo, rwh_ref[...], preferred_element_type=F32)
                  + jnp.dot(u_hi, rwl_ref[...], preferred_element_type=F32)) + rb_ref[...]
        lane = lax.broadcasted_iota(jnp.int32, logits.shape, 1)
        logits = jnp.where(lane < N_EXPERTS, logits, -jnp.inf)
        idx, wts = _top2(logits)
        outs[2][...] = idx
        outs[3][...] = wts


def _epilogue(geom, h, f, rows, mods, gate_slot, ln_g, ln_b, next_mods=None, next_slots=None, router=None):
    n_tiles = rows // ROW_TILE
    row = pl.BlockSpec((ROW_TILE, D_MODEL), lambda i: (i, 0))
    vec = pl.BlockSpec((1, D_MODEL), lambda i: (0, 0))
    lat_tiles = None
    if isinstance(h, tuple):
        lat_tiles = geom.lat_tiles
        h_specs, h_args = _split_row_specs(lat_tiles), list(h)
    else:
        h_specs, h_args = [row], [h]
    in_specs = h_specs + [row, _mod_spec(geom, gate_slot), vec, vec]
    args = h_args + [f, mods, ln_g.reshape(1, D_MODEL), ln_b.reshape(1, D_MODEL)]
    out_specs = [row]
    out_shape = [jax.ShapeDtypeStruct((rows, D_MODEL), F32)]
    modulate = next_mods is not None
    if modulate:
        in_specs += [_mod_spec(geom, next_slots[0]), _mod_spec(geom, next_slots[1])]
        args += [next_mods, next_mods]
        if router is not None:
            out_specs.append(pl.BlockSpec((ROW_TILE, HALF_D), lambda i: (i, 0)))
            out_shape.append(jax.ShapeDtypeStruct((rows, HALF_D), jnp.uint32))
        else:
            out_specs.append(row)
            out_shape.append(jax.ShapeDtypeStruct((rows, D_MODEL), BF16))
    if router is not None:
        rw_hi, rw_lo, rb = router
        small = pl.BlockSpec((D_MODEL, LANES), lambda i: (0, 0))
        in_specs += [small, small, pl.BlockSpec((1, LANES), lambda i: (0, 0))]
        args += [rw_hi, rw_lo, rb]
        lane_blk = pl.BlockSpec((ROW_TILE, LANES), lambda i: (i, 0))
        out_specs += [lane_blk, lane_blk]
        out_shape += [jax.ShapeDtypeStruct((rows, LANES), jnp.int32),
                      jax.ShapeDtypeStruct((rows, LANES), F32)]
    return pl.pallas_call(
        functools.partial(_epilogue_kernel, modulate=modulate, router=router is not None, lat_tiles=lat_tiles),
        grid=(n_tiles,),
        in_specs=in_specs,
        out_specs=out_specs,
        out_shape=out_shape,
        compiler_params=_cparams(("arbitrary",)),
        name="epilogue",
    )(*args)


GATHER_ROWS = 256


def _row_copy(src_hbm, dst_vmem, sem, src_row, dst_row):
    return pltpu.make_async_copy(src_hbm.at[pl.ds(src_row, 1), :], dst_vmem.at[pl.ds(dst_row, 1), :], sem)


def _issue_rows(src_ref, dst_refs, sem, idx_fn, n_rows):
    def body(r, carry):
        for p, dst in enumerate(dst_refs):
            _row_copy(src_ref, dst, sem, idx_fn(len(dst_refs) * r + p), r).start(priority=p)
        return carry

    lax.fori_loop(0, n_rows, body, 0, unroll=8)


def _drain_rows(src_ref, dst_refs, sem, n_rows):
    for dst in dst_refs:
        pltpu.make_async_copy(src_ref.at[pl.ds(0, n_rows), :], dst, sem).wait()


def _double_buffered_rows(src_ref, bufs_of_slot, sem, cur_idx_ref, nxt_idx_ref, n_rows):
    i = pl.program_id(0)
    slot = i % 2

    @pl.when(i == 0)
    def _():
        _issue_rows(src_ref, bufs_of_slot(0), sem.at[0], lambda q: cur_idx_ref[0, 0, q], n_rows)

    @pl.when(i + 1 < pl.num_programs(0))
    def _():
        _issue_rows(src_ref, bufs_of_slot(1 - slot), sem.at[1 - slot], lambda q: nxt_idx_ref[0, 0, q], n_rows)

    _drain_rows(src_ref, bufs_of_slot(slot), sem.at[slot], n_rows)
    return slot


def _idx_specs(per_step, steps):
    cur = pl.BlockSpec((1, 1, per_step), lambda i: (i, 0, 0), memory_space=pltpu.SMEM)
    nxt = pl.BlockSpec((1, 1, per_step), lambda i: (jnp.minimum(i + 1, steps - 1), 0, 0), memory_space=pltpu.SMEM)
    return cur, nxt


def _gather_kernel(idx_ref, nxt_ref, src_ref, o_ref, buf_ref, sem):
    half = GATHER_ROWS // 2

    def bufs(slot):
        return (buf_ref.at[slot, 0], buf_ref.at[slot, 1])

    slot = _double_buffered_rows(src_ref, bufs, sem, idx_ref, nxt_ref, half)
    for p in range(2):
        lo, hi = _unpack_bf16_pairs(buf_ref[slot, p])
        o_ref[p, :, :HALF_D] = lo
        o_ref[p, :, HALF_D:] = hi


def _gather_rows(src, slot_token):
    n_slots = slot_token.shape[0]
    width = src.shape[1]
    steps = n_slots // GATHER_ROWS
    half = GATHER_ROWS // 2
    idx = jnp.transpose(slot_token.reshape(steps, 2, half), (0, 2, 1)).reshape(steps, 1, GATHER_ROWS)
    cur, nxt = _idx_specs(GATHER_ROWS, steps)
    out = pl.pallas_call(
        _gather_kernel,
        grid=(steps,),
        in_specs=[cur, nxt, pl.BlockSpec(memory_space=pl.ANY)],
        out_specs=pl.BlockSpec((None, 2, half, 2 * width), lambda i: (i, 0, 0, 0)),
        out_shape=jax.ShapeDtypeStruct((steps, 2, half, 2 * width), BF16),
        scratch_shapes=[pltpu.VMEM((2, 2, half, width), src.dtype), pltpu.SemaphoreType.DMA((2,))],
        compiler_params=_cparams(("arbitrary",)),
        name="moe_gather",
    )(idx, idx, src)
    return out.reshape(n_slots, 2 * width)


def _tile_idx(w, nused_ref):
    return jnp.minimum(w, nused_ref[0] - 1)


def _fresh_expert(te_ref, w):
    return (w == 0) | (te_ref[w] != te_ref[jnp.maximum(w - 1, 0)])


def _moe_up_kernel(te_ref, nused_ref, a_ref, w1_ref, w3_ref, o_ref, w1b_ref, w3b_ref):
    w = pl.program_id(1)
    used = w < nused_ref[0]

    @pl.when(used & _fresh_expert(te_ref, w))
    def _():
        w1b_ref[...] = w1_ref[...].astype(BF16)
        w3b_ref[...] = w3_ref[...].astype(BF16)

    @pl.when(used)
    def _():
        a = a_ref[...]
        h1 = jnp.dot(a, w1b_ref[...], preferred_element_type=F32)
        h3 = jnp.dot(a, w3b_ref[...], preferred_element_type=F32)
        o_ref[...] = (_silu(h1) * h3).astype(o_ref.dtype)

    @pl.when(jnp.logical_not(used))
    def _():
        o_ref[...] = jnp.zeros_like(o_ref)


MOE_UP_VMEM = 59 * 1024 * 1024


def _moe_up(a_sorted, w1, w3, tile_expert, n_used, tf=512):
    n_slots, k = a_sorted.shape
    n_tiles = n_slots // MOE_TM
    f = w1.shape[2]
    w_map = lambda j, w, te, nu: (te[_tile_idx(w, nu)], 0, j)
    grid_spec = pltpu.PrefetchScalarGridSpec(
        num_scalar_prefetch=2,
        grid=(f // tf, n_tiles),
        in_specs=[pl.BlockSpec((MOE_TM, k), lambda j, w, te, nu: (_tile_idx(w, nu), 0)),
                  pl.BlockSpec((None, k, tf), w_map),
                  pl.BlockSpec((None, k, tf), w_map)],
        out_specs=pl.BlockSpec((MOE_TM, tf), lambda j, w, te, nu: (w, j)),
        scratch_shapes=[pltpu.VMEM((k, tf), BF16), pltpu.VMEM((k, tf), BF16)],
    )
    return pl.pallas_call(
        _moe_up_kernel,
        grid_spec=grid_spec,
        out_shape=jax.ShapeDtypeStruct((n_slots, f), BF16),
        compiler_params=_cparams(("arbitrary", "arbitrary"), MOE_UP_VMEM),
        name="moe_up",
    )(tile_expert, n_used, a_sorted, w1, w3)


def _moe_down_kernel(te_ref, nused_ref, a_ref, w_ref, o_ref, wb_ref):
    w = pl.program_id(1)
    used = w < nused_ref[0]

    @pl.when(used & _fresh_expert(te_ref, w))
    def _():
        wb_ref[...] = w_ref[...].astype(BF16)

    @pl.when(used)
    def _():
        o_ref[...] = jnp.dot(a_ref[...], wb_ref[...], preferred_element_type=F32)

    @pl.when(jnp.logical_not(used))
    def _():
        o_ref[...] = jnp.zeros_like(o_ref)


def _moe_down(hid, w2, tile_expert, n_used, tn=1024):
    n_slots, k = hid.shape
    n_tiles = n_slots // MOE_TM
    n = w2.shape[2]
    grid_spec = pltpu.PrefetchScalarGridSpec(
        num_scalar_prefetch=2,
        grid=(n // tn, n_tiles),
        in_specs=[pl.BlockSpec((MOE_TM, k), lambda j, w, te, nu: (_tile_idx(w, nu), 0)),
                  pl.BlockSpec((None, k, tn), lambda j, w, te, nu: (te[_tile_idx(w, nu)], 0, j))],
        out_specs=pl.BlockSpec((MOE_TM, tn), lambda j, w, te, nu: (w, j)),
        scratch_shapes=[pltpu.VMEM((k, tn), BF16)],
    )
    return pl.pallas_call(
        _moe_down_kernel,
        grid_spec=grid_spec,
        out_shape=jax.ShapeDtypeStruct((n_slots, n), F32),
        compiler_params=_cparams(("arbitrary", "arbitrary")),
        name="moe_down",
    )(tile_expert, n_used, hid, w2)


COMBINE_ROWS = 128


def _combine_kernel(pos_ref, nxt_ref, h_ref, wts_ref, gate_ref, lg_ref, lb_ref, y_ref, o_ref, buf_ref, sem):
    def bufs(slot):
        return (buf_ref.at[slot, 0], buf_ref.at[slot, 1])

    slot = _double_buffered_rows(y_ref, bufs, sem, pos_ref, nxt_ref, COMBINE_ROWS)
    wts = wts_ref[...]
    f = buf_ref[slot, 0] * wts[:, 0:1] + buf_ref[slot, 1] * wts[:, 1:2]
    o_ref[...] = _residual_norm(h_ref[...], f, gate_ref[...], lg_ref[...], lb_ref[...])


def _moe_combine(geom, h, y_sorted, pos, wts, mods, gate_slot, ln_g, ln_b):
    rows = geom.n_lat
    steps = rows // COMBINE_ROWS
    per_mod = ROW_TILE // COMBINE_ROWS
    row = pl.BlockSpec((COMBINE_ROWS, D_MODEL), lambda i: (i, 0))
    vec = pl.BlockSpec((1, D_MODEL), lambda i: (0, 0))
    cur, nxt = _idx_specs(2 * COMBINE_ROWS, steps)
    pos3 = pos.reshape(steps, 1, 2 * COMBINE_ROWS)
    return pl.pallas_call(
        _combine_kernel,
        grid=(steps,),
        in_specs=[cur, nxt, row,
                  pl.BlockSpec((COMBINE_ROWS, LANES), lambda i: (i, 0)),
                  pl.BlockSpec((None, 1, D_MODEL), lambda i: (geom.mod_row(i // per_mod) * 6 + gate_slot, 0, 0)),
                  vec, vec,
                  pl.BlockSpec(memory_space=pl.ANY)],
        out_specs=row,
        out_shape=jax.ShapeDtypeStruct((rows, D_MODEL), F32),
        scratch_shapes=[pltpu.VMEM((2, 2, COMBINE_ROWS, D_MODEL), F32), pltpu.SemaphoreType.DMA((2,))],
        compiler_params=_cparams(("arbitrary",)),
        name="moe_combine",
    )(pos3, pos3, h, wts, mods, ln_g.reshape(1, D_MODEL), ln_b.reshape(1, D_MODEL), y_sorted)


def _route(top_idx, n_tokens):
    flat = top_idx.reshape(-1)
    onehot = (jnp.arange(N_EXPERTS, dtype=jnp.int32)[:, None] == flat[None, :]).astype(jnp.int32)
    blk = 2 * ROW_TILE
    inner = jnp.cumsum(onehot.reshape(N_EXPERTS, -1, blk), axis=2)
    blk_tot = inner[:, :, -1]
    blk_off = jnp.cumsum(blk_tot, axis=1) - blk_tot
    csum = (inner + blk_off[:, :, None]).reshape(N_EXPERTS, -1)
    rank = jnp.sum(csum * onehot, axis=0) - 1
    counts = csum[:, -1]
    tiles_per = (counts + MOE_TM - 1) // MOE_TM
    tile_end = jnp.cumsum(tiles_per)
    tile_start = tile_end - tiles_per
    pos = (jnp.sum(onehot * (tile_start * MOE_TM)[:, None], axis=0) + rank).astype(jnp.int32)
    n_tiles = (2 * n_tokens) // MOE_TM + N_EXPERTS
    n_used = tile_end[-1].astype(jnp.int32)
    tile_ids = jnp.arange(n_tiles, dtype=jnp.int32)
    tile_expert = jnp.sum((tile_ids[:, None] >= tile_end[None, :]).astype(jnp.int32), axis=1)
    tile_expert = jnp.minimum(tile_expert, N_EXPERTS - 1).astype(jnp.int32)
    slot_token = jnp.zeros((n_tiles * MOE_TM,), jnp.int32).at[pos].set(
        jnp.arange(2 * n_tokens, dtype=jnp.int32) // 2)
    return pos, slot_token, tile_expert, n_used.reshape(1)


def kernel(x, c, ctx, c_ctx, ada_w, ada_b, w_in, mamba_conv_w, mamba_conv_b, dt_bias_fwd, dt_bias_bwd, a_log_fwd, a_log_bwd, d_skip, ssm_norm_w, conf_conv_w, conf_conv_b, conf_ln_g, conf_ln_b, w_out, ln1_g, ln1_b, ln2_g, ln2_b, ffn_w1, ffn_w3, ffn_w2, router_w, router_b, moe_w1, moe_w3, moe_w2):
    bsz, seq, d = x.shape
    ctx_len = ctx.shape[1]
    geom = _Geom(bsz, seq, ctx_len)
    n_all, n_lat = geom.n_all, geom.n_lat

    mod_rows = SUBLANES
    cond_rows = jnp.zeros((mod_rows, d), F32).at[:bsz].set(c).at[bsz].set(c_ctx)
    mods_all = _ada_mods(cond_rows, ada_w, ada_b).reshape(DEPTH, mod_rows * 6, 1, d)

    h = (x.reshape(n_lat, d), ctx.reshape(bsz * ctx_len, d))
    u = _pack_modulate(geom, h[0], h[1], mods_all[0])

    tm_all = _pick_tile(n_all, 1056)
    tm_lat = _pick_tile(n_lat, 1024)

    w_main_all = jnp.concatenate([w_in[:, :, :OFF_DT], w_in[:, :, OFF_X:]], axis=2).astype(BF16)

    for i in range(DEPTH):
        mods = mods_all[i]
        last = i == DEPTH - 1
        w_main = w_main_all[i]
        w_dt = jnp.pad(w_in[i][:, OFF_DT:OFF_X], ((0, 0), (0, LANES - 2 * SSM_HEADS))).astype(BF16)
        w_o = w_out[i].astype(BF16)

        proj = _matmul([(u, w_main, 0)], n_all, tm_all, 1024, F32, "in_proj")
        dtr = _matmul([(u, w_dt, 0)], n_all, tm_all, LANES, F32, "dt_proj")
        dtt = jnp.transpose(dtr[:, :2 * SSM_HEADS])

        xs, bc = _ssm_conv(geom, proj, mamba_conv_w[i], mamba_conv_b[i])
        y_b = _ssd_pass(geom, xs, bc, dtr, dtt, dt_bias_bwd[i], a_log_bwd[i], rev=True)
        dsk = jnp.repeat(d_skip[i], SSM_HEADDIM).reshape(1, D_SSM)
        y_ssm = _ssd_pass(geom, xs, bc, dtr, dtt, dt_bias_fwd[i], a_log_fwd[i], rev=False,
                          final_args=(y_b, proj, dsk, ssm_norm_w[i].reshape(1, D_SSM)))

        if i % 2 == 0:
            v = _conformer_rows(geom, proj, conf_conv_w[i], conf_conv_b[i], conf_ln_g[i], conf_ln_b[i])
        else:
            conv = _conformer_cols_conv(geom, proj, conf_conv_w[i], conf_conv_b[i])
            v = _ln_swish(conv, conf_ln_g[i], conf_ln_b[i])

        rows = n_lat if last else n_all
        tm = tm_lat if last else tm_all
        mix = _matmul([(y_ssm, w_o, 0), (v, w_o, 1)], rows, tm, 1024, F32, "out_proj")

        j = i // 2
        if i % 2 == 0:
            h, u2 = _epilogue(geom, h, mix, rows, mods, 2, ln1_g[i], ln1_b[i], mods, (3, 4))
            w2 = jnp.concatenate([ffn_w2[j].astype(BF16), jnp.zeros((FFN_PAD - FFN_DENSE, d), BF16)], axis=0)
            hid = _gated_up(u2, ffn_w1[j], ffn_w3[j], rows, _pick_tile(rows, 1408), 256, FFN_PAD)
            f = _matmul_ktiled(hid, w2, rows, tm, 2048, 1024, "ffn_down")
            if last:
                h = _epilogue(geom, h, f, rows, mods, 5, ln2_g[i], ln2_b[i])[0]
            else:
                h, u = _epilogue(geom, h, f, rows, mods, 5, ln2_g[i], ln2_b[i], mods_all[i + 1], (0, 1))
        else:
            rw = jnp.pad(router_w[j], ((0, 0), (0, LANES - N_EXPERTS)))
            rw_hi = rw.astype(BF16)
            rw_lo = (rw - rw_hi.astype(F32)).astype(BF16)
            rb = jnp.pad(router_b[j], (0, LANES - N_EXPERTS)).reshape(1, LANES)
            assert last, "a routed layer is only supported as the final layer"
            h, u2p, top_idx, top_w = _epilogue(geom, h, mix, rows, mods, 2, ln1_g[i], ln1_b[i], mods, (3, 4),
                                                  router=(rw_hi, rw_lo, rb))
            pos, slot_token, tile_expert, n_used = _route(top_idx[:, :2], rows)
            a_sorted = _gather_rows(u2p, slot_token)
            hid = _moe_up(a_sorted, moe_w1[j], moe_w3[j], tile_expert, n_used)
            y_sorted = _moe_down(hid, moe_w2[j], tile_expert, n_used)
            h = _moe_combine(geom, h, y_sorted, pos, top_w, mods, 5, ln2_g[i], ln2_b[i])

    return h[:n_lat].reshape(bsz, seq, d)
```

```python
import functools

import jax
import jax.numpy as jnp
from jax import lax
from jax.experimental import pallas as pl
from jax.experimental.pallas import tpu as pltpu

F32 = jnp.float32
BF16 = jnp.bfloat16

D_MODEL = 4096
DEPTH = 2
GRID_W = 64
D_SSM = 2048
D_CONF = 2048
SSM_HEADDIM = 64
SSM_HEADS = 32
SSM_GROUPS = 8
HEADS_PER_GROUP = 4
SSM_STATE = 128
SSM_CONV = 5
CHUNK = 128
GROUP_W = HEADS_PER_GROUP * SSM_HEADDIM
CONF_WIDTH = 31
CONF_HALF = CONF_WIDTH // 2
OFF_DT = D_SSM
OFF_X = OFF_DT + 2 * SSM_HEADS
N_MAIN = 10240
COL_X, COL_BC, COL_VAL, COL_GATE = 1, 2, 3, 4
FFN_DENSE = 11008
FFN_PAD = 11264
N_EXPERTS = 8
FFN_EXPERT = 3584
LN_EPS = 1e-5
ALPHA = (2 * DEPTH) ** 0.25

LANES = 128
SUBLANES = 8
ROW_TILE = 256
MOE_TM = 512
VMEM_LIMIT = 56 * 1024 * 1024


def _cparams(sem, vmem=VMEM_LIMIT):
    return pltpu.CompilerParams(dimension_semantics=sem, vmem_limit_bytes=vmem)


def _sigmoid(x):
    return 1.0 / (1.0 + jnp.exp(-x))


def _silu(x):
    return x * _sigmoid(x)


def _softplus(x):
    return jnp.maximum(x, 0.0) + jnp.log(1.0 + jnp.exp(-jnp.abs(x)))


def _standardize(x):
    mu = jnp.mean(x, axis=-1, keepdims=True)
    xc = x - mu
    var = jnp.mean(xc * xc, axis=-1, keepdims=True)
    return xc * lax.rsqrt(var + LN_EPS)


def _pick_tile(n, target, mult=16):
    best = None
    for t in range(mult, min(n, target) + 1, mult):
        if n % t == 0:
            best = t
    assert best is not None, (n, target)
    return best


def _ada_kernel(c_ref, w_ref, b_ref, o_ref):
    cond = _silu(c_ref[...]).astype(BF16)
    o_ref[...] = jnp.dot(cond, w_ref[...].astype(BF16), preferred_element_type=F32) + b_ref[...]


def _ada_mods(cond_rows, ada_w, ada_b):
    depth, d, n = ada_w.shape
    rows = cond_rows.shape[0]
    tn = 1024
    return pl.pallas_call(
        _ada_kernel,
        grid=(depth, n // tn),
        in_specs=[
            pl.BlockSpec((rows, d), lambda l, j: (0, 0)),
            pl.BlockSpec((None, d, tn), lambda l, j: (l, 0, j)),
            pl.BlockSpec((None, 1, tn), lambda l, j: (l, 0, j)),
        ],
        out_specs=pl.BlockSpec((None, rows, tn), lambda l, j: (l, 0, j)),
        out_shape=jax.ShapeDtypeStruct((depth, rows, n), F32),
        compiler_params=_cparams(("arbitrary", "arbitrary")),
        name="ada_mods",
    )(cond_rows, ada_w, ada_b.reshape(depth, 1, n))


class _Geom:
    def __init__(self, bsz, seq, ctx_len):
        self.bsz, self.seq, self.ctx = bsz, seq, ctx_len
        self.n_lat = bsz * seq
        self.n_all = self.n_lat + bsz * ctx_len
        assert seq % ROW_TILE == 0 and ctx_len % ROW_TILE == 0 and seq % GRID_W == 0
        self.lat_tiles = self.n_lat // ROW_TILE
        self.tiles_per_seq = seq // ROW_TILE
        self.tiles_per_ctx = ctx_len // ROW_TILE

    def mod_row(self, i):
        return jnp.where(i < self.lat_tiles, i // self.tiles_per_seq, self.bsz)

    def seq_edges(self, i):
        lat = i < self.lat_tiles
        j = i - self.lat_tiles
        first = jnp.where(lat, i % self.tiles_per_seq == 0, j % self.tiles_per_ctx == 0)
        last = jnp.where(lat, i % self.tiles_per_seq == self.tiles_per_seq - 1,
                         j % self.tiles_per_ctx == self.tiles_per_ctx - 1)
        return first, last


def _mod_spec(geom, slot):
    return pl.BlockSpec((None, 1, D_MODEL), lambda i: (geom.mod_row(i) * 6 + slot, 0, 0))


def _split_row_specs(lat_tiles):
    return [pl.BlockSpec((ROW_TILE, D_MODEL), lambda i: (jnp.minimum(i, lat_tiles - 1), 0)),
            pl.BlockSpec((ROW_TILE, D_MODEL), lambda i: (jnp.maximum(i - lat_tiles, 0), 0))]


def _pack_mod_kernel(x_ref, c_ref, sh_ref, sc_ref, u_ref, *, lat_tiles):
    v = jnp.where(pl.program_id(0) < lat_tiles, x_ref[...], c_ref[...])
    u_ref[...] = (_standardize(v) * (1.0 + sc_ref[...]) + sh_ref[...]).astype(BF16)


def _pack_modulate(geom, x2, ctx2, mods):
    n_tiles = geom.n_all // ROW_TILE
    return pl.pallas_call(
        functools.partial(_pack_mod_kernel, lat_tiles=geom.lat_tiles),
        grid=(n_tiles,),
        in_specs=_split_row_specs(geom.lat_tiles) + [_mod_spec(geom, 0), _mod_spec(geom, 1)],
        out_specs=pl.BlockSpec((ROW_TILE, D_MODEL), lambda i: (i, 0)),
        out_shape=jax.ShapeDtypeStruct((geom.n_all, D_MODEL), BF16),
        compiler_params=_cparams(("arbitrary",)),
        name="pack_modulate",
    )(x2, ctx2, mods, mods)


W_PREP_ROWS = 256


def _w_in_prep_kernel(w_ref, main_ref, dt_ref):
    main_ref[:, :OFF_DT] = w_ref[:, :OFF_DT].astype(BF16)
    main_ref[:, OFF_DT:] = w_ref[:, OFF_X:].astype(BF16)
    dt_ref[...] = jnp.zeros_like(dt_ref)
    dt_ref[:, :OFF_X - OFF_DT] = w_ref[:, OFF_DT:OFF_X].astype(BF16)


def _w_in_prep(w_in):
    depth, k, n = w_in.shape
    return pl.pallas_call(
        _w_in_prep_kernel,
        grid=(depth, k // W_PREP_ROWS),
        in_specs=[pl.BlockSpec((None, W_PREP_ROWS, n), lambda l, r: (l, r, 0))],
        out_specs=[pl.BlockSpec((None, W_PREP_ROWS, N_MAIN), lambda l, r: (l, r, 0)),
                   pl.BlockSpec((None, W_PREP_ROWS, LANES), lambda l, r: (l, r, 0))],
        out_shape=[jax.ShapeDtypeStruct((depth, k, N_MAIN), BF16),
                   jax.ShapeDtypeStruct((depth, k, LANES), BF16)],
        compiler_params=_cparams(("arbitrary", "arbitrary")),
        name="w_in_prep",
    )(w_in)


def _mm_kernel(*refs, n_pairs):
    o_ref = refs[-1]
    acc = jnp.dot(refs[0][...], refs[1][...], preferred_element_type=F32)
    for p in range(1, n_pairs):
        acc = acc + jnp.dot(refs[2 * p][...], refs[2 * p + 1][...], preferred_element_type=F32)
    o_ref[...] = acc.astype(o_ref.dtype)


def _matmul(pairs, rows, tm, tn, out_dtype, name):
    n = pairs[0][1].shape[-1]
    assert rows % tm == 0 and n % tn == 0
    in_specs, args = [], []
    for a, w, row_blk in pairs:
        k = a.shape[1]
        if isinstance(row_blk, tuple):
            w_spec = pl.BlockSpec((None, k, tn), lambda i, j, rb=row_blk: (rb[0], rb[1], j))
        else:
            w_spec = pl.BlockSpec((k, tn), lambda i, j, rb=row_blk: (rb, j))
        in_specs += [pl.BlockSpec((tm, k), lambda i, j: (i, 0)), w_spec]
        args += [a, w]
    return pl.pallas_call(
        functools.partial(_mm_kernel, n_pairs=len(pairs)),
        grid=(rows // tm, n // tn),
        in_specs=in_specs,
        out_specs=pl.BlockSpec((tm, tn), lambda i, j: (i, j)),
        out_shape=jax.ShapeDtypeStruct((rows, n), out_dtype),
        compiler_params=_cparams(("arbitrary", "arbitrary")),
        name=name,
    )(*args)


def _gated_up_kernel(a_ref, w1_ref, w3_ref, o_ref, *, valid_tiles):
    a = a_ref[...]
    h1 = jnp.dot(a, w1_ref[...].astype(BF16), preferred_element_type=F32)
    h3 = jnp.dot(a, w3_ref[...].astype(BF16), preferred_element_type=F32)
    keep = pl.program_id(1) < valid_tiles
    o_ref[...] = jnp.where(keep, _silu(h1) * h3, 0.0).astype(o_ref.dtype)


def _gated_up(a, w1, w3, rows, tm, tn, n_out):
    k, n_valid = w1.shape
    n = n_out
    assert n_valid % tn == 0 and n % tn == 0
    valid_tiles = n_valid // tn
    w_spec = pl.BlockSpec((k, tn), lambda i, j: (0, jnp.minimum(j, valid_tiles - 1)))
    return pl.pallas_call(
        functools.partial(_gated_up_kernel, valid_tiles=valid_tiles),
        grid=(rows // tm, n // tn),
        in_specs=[pl.BlockSpec((tm, k), lambda i, j: (i, 0)), w_spec, w_spec],
        out_specs=pl.BlockSpec((tm, tn), lambda i, j: (i, j)),
        out_shape=jax.ShapeDtypeStruct((rows, n), BF16),
        compiler_params=_cparams(("arbitrary", "arbitrary")),
        name="ffn_up",
    )(a, w1, w3)


def _mm_acc_kernel(a_ref, w_ref, o_ref, acc_ref, *, k_valid):
    k = pl.program_id(2)
    last = pl.num_programs(2) - 1
    tk = w_ref.shape[0]

    @pl.when(k == 0)
    def _():
        acc_ref[...] = jnp.zeros_like(acc_ref)

    def accumulate(w):
        acc_ref[...] += jnp.dot(a_ref[...], w, preferred_element_type=F32)

    tail = k_valid % tk
    if tail == 0:
        accumulate(w_ref[...])
    else:
        @pl.when(k < last)
        def _():
            accumulate(w_ref[...])

        @pl.when(k == last)
        def _():
            row = lax.broadcasted_iota(jnp.int32, w_ref.shape, 0)
            accumulate(jnp.where(row < tail, w_ref[...], jnp.zeros_like(w_ref)))

    @pl.when(k == last)
    def _():
        o_ref[...] = acc_ref[...]


def _matmul_ktiled(a, w, rows, tm, tn, tk, name):
    kk = a.shape[1]
    k_valid, n = w.shape
    assert rows % tm == 0 and n % tn == 0 and kk % tk == 0 and kk - k_valid < tk
    return pl.pallas_call(
        functools.partial(_mm_acc_kernel, k_valid=k_valid),
        grid=(rows // tm, n // tn, kk // tk),
        in_specs=[pl.BlockSpec((tm, tk), lambda i, j, k: (i, k)),
                  pl.BlockSpec((tk, tn), lambda i, j, k: (k, j))],
        out_specs=pl.BlockSpec((tm, tn), lambda i, j, k: (i, j)),
        out_shape=jax.ShapeDtypeStruct((rows, n), F32),
        scratch_shapes=[pltpu.VMEM((tm, tn), F32)],
        compiler_params=_cparams(("arbitrary", "arbitrary", "arbitrary")),
        name=name,
    )(a, w)


def _conv5_kernel(xm_ref, xp_ref, xn_ref, bm_ref, bp_ref, bn_ref, w_ref, b_ref, xo_ref, bco_ref, scr_ref,
                  *, geom):
    i = pl.program_id(0)
    first, last = geom.seq_edges(i)
    half = D_SSM
    pad = SUBLANES
    for part, (m_ref, p_ref, n_ref) in enumerate(((xm_ref, xp_ref, xn_ref), (bm_ref, bp_ref, bn_ref))):
        lo = part * half
        scr_ref[0:pad, lo:lo + half] = jnp.where(first, 0.0, p_ref[...])
        scr_ref[pad:pad + ROW_TILE, lo:lo + half] = m_ref[...]
        scr_ref[pad + ROW_TILE:2 * pad + ROW_TILE, lo:lo + half] = jnp.where(last, 0.0, n_ref[...])
    rows_blk, lane_blk = 64, 512
    base = pad - SSM_CONV // 2
    for c0 in range(0, 2 * half, lane_blk):
        wc = w_ref[:, c0:c0 + lane_blk]
        bias = b_ref[:, c0:c0 + lane_blk]
        for r0 in range(0, ROW_TILE, rows_blk):
            acc = bias
            for k in range(SSM_CONV):
                acc = acc + scr_ref[base + r0 + k:base + r0 + k + rows_blk, c0:c0 + lane_blk] * wc[k:k + 1, :]
            val = _silu(acc)
            if c0 < half:
                xo_ref[r0:r0 + rows_blk, c0:c0 + lane_blk] = val
            else:
                bco_ref[r0:r0 + rows_blk, c0 - half:c0 - half + lane_blk] = val.astype(BF16)


def _ssm_conv(geom, proj, conv_w, conv_b):
    n_tiles = geom.n_all // ROW_TILE
    sub = ROW_TILE // SUBLANES
    last_blk = geom.n_all // SUBLANES - 1

    def main(col):
        return pl.BlockSpec((ROW_TILE, D_SSM), lambda i: (i, col))

    def prev(col):
        return pl.BlockSpec((SUBLANES, D_SSM), lambda i: (jnp.maximum(i * sub - 1, 0), col))

    def nxt(col):
        return pl.BlockSpec((SUBLANES, D_SSM), lambda i: (jnp.minimum((i + 1) * sub, last_blk), col))

    width = 2 * D_SSM
    return pl.pallas_call(
        functools.partial(_conv5_kernel, geom=geom),
        grid=(n_tiles,),
        in_specs=[main(COL_X), prev(COL_X), nxt(COL_X), main(COL_BC), prev(COL_BC), nxt(COL_BC),
                  pl.BlockSpec((SSM_CONV, width), lambda i: (0, 0)),
                  pl.BlockSpec((1, width), lambda i: (0, 0))],
        out_specs=[pl.BlockSpec((ROW_TILE, D_SSM), lambda i: (i, 0)),
                   pl.BlockSpec((ROW_TILE, D_SSM), lambda i: (i, 0))],
        out_shape=[jax.ShapeDtypeStruct((geom.n_all, D_SSM), F32),
                   jax.ShapeDtypeStruct((geom.n_all, D_SSM), BF16)],
        scratch_shapes=[pltpu.VMEM((ROW_TILE + 2 * SUBLANES, width), F32)],
        compiler_params=_cparams(("arbitrary",)),
        name="ssm_conv",
    )(proj, proj, proj, proj, proj, proj, conv_w, conv_b.reshape(1, width))


def _split3_dot(lhs_fn, v):
    v1 = v.astype(BF16)
    r1 = v - v1.astype(F32)
    v2 = r1.astype(BF16)
    v3 = (r1 - v2.astype(F32)).astype(BF16)
    return lhs_fn(v1) + lhs_fn(v2) + lhs_fn(v3)


def _ssd_kernel(*refs, rev, final):
    if final:
        (x_ref, bc_ref, dtr_ref, dtt_ref, pbr_ref, pbc_ref, alr_ref, alc_ref,
         yb_ref, z_ref, dsk_ref, nw_ref, o_ref, s_ref) = refs
    else:
        x_ref, bc_ref, dtr_ref, dtt_ref, pbr_ref, pbc_ref, alr_ref, alc_ref, o_ref, s_ref = refs
    step = pl.program_id(1)

    @pl.when(step == 0)
    def _():
        s_ref[...] = jnp.zeros_like(s_ref)

    n = CHUNK
    row = lax.broadcasted_iota(jnp.int32, (n, n), 0)
    col = lax.broadcasted_iota(jnp.int32, (n, n), 1)
    tri = (row <= col) if rev else (row >= col)
    cum_l = tri.astype(BF16)
    cum_r = jnp.logical_not(tri) | (row == col)
    cum_r = cum_r.astype(BF16)
    h_off = SSM_HEADS if rev else 0
    edge = 0 if rev else n - 1

    dt = _softplus(dtr_ref[...] + pbr_ref[...])
    a = -dt * jnp.exp(alr_ref[...])
    dtt = _softplus(dtt_ref[...] + pbc_ref[...])
    at = -dtt * jnp.exp(alc_ref[...])
    acum = _split3_dot(lambda v: jnp.dot(cum_l, v, preferred_element_type=F32), a)
    acum_t = _split3_dot(lambda v: jnp.dot(v, cum_r, preferred_element_type=F32), at)

    lane_lo = lax.broadcasted_iota(jnp.int32, (n, LANES), 1) < SSM_HEADDIM
    lane_w = lax.broadcasted_iota(jnp.int32, (n, GROUP_W), 1)

    for g in range(SSM_GROUPS):
        b_g = bc_ref[:, g * SSM_STATE:(g + 1) * SSM_STATE]
        c_g = bc_ref[:, SSM_GROUPS * SSM_STATE + g * SSM_STATE:SSM_GROUPS * SSM_STATE + (g + 1) * SSM_STATE]
        x_g = x_ref[:, g * GROUP_W:(g + 1) * GROUP_W]
        x_gb = x_g.astype(BF16)
        s_g = s_ref[g]
        cb = lax.dot_general(c_g, b_g, (((1,), (1,)), ((), ())), preferred_element_type=F32)

        m_parts, colbs, dtcols = [], [], []
        for e in range(HEADS_PER_GROUP):
            h = g * HEADS_PER_GROUP + e
            colb = jnp.broadcast_to(acum[:, h_off + h:h_off + h + 1], (n, n))
            rowb = jnp.broadcast_to(acum_t[h:h + 1, :], (n, n))
            dtrow = jnp.broadcast_to(dtt[h:h + 1, :], (n, n))
            decay = jnp.exp(jnp.where(tri, colb - rowb, -jnp.inf))
            m_parts.append((cb * decay * dtrow).astype(BF16))
            colbs.append(colb)
            dtcols.append(jnp.broadcast_to(dt[:, h_off + h:h_off + h + 1], (n, n)))

        ecols, wcols, etots = [], [], []
        for pair in range(HEADS_PER_GROUP // 2):
            colb = jnp.where(lane_lo, colbs[2 * pair], colbs[2 * pair + 1])
            dtcol = jnp.where(lane_lo, dtcols[2 * pair], dtcols[2 * pair + 1])
            ecol = jnp.exp(colb)
            tot = jnp.broadcast_to(colb[edge:edge + 1, :], (n, n))
            ecols.append(ecol)
            wcols.append(jnp.exp(tot - colb) * dtcol)
            etots.append(ecol[edge:edge + 1, :])
        scale_e = jnp.concatenate(ecols, axis=1)
        scale_w = jnp.concatenate(wcols, axis=1)
        scale_t = jnp.concatenate(etots, axis=1)

        y = None
        for pair in range(HEADS_PER_GROUP // 2):
            e0, e1 = 2 * pair, 2 * pair + 1
            lhs = jnp.concatenate([m_parts[e0], m_parts[e1]], axis=1)
            r0 = jnp.where((lane_w // SSM_HEADDIM) == e0, x_gb, jnp.zeros_like(x_gb))
            r1 = jnp.where((lane_w // SSM_HEADDIM) == e1, x_gb, jnp.zeros_like(x_gb))
            part = jnp.dot(lhs, jnp.concatenate([r0, r1], axis=0), preferred_element_type=F32)
            y = part if y is None else y + part
        y = y + jnp.dot(c_g, s_g.astype(BF16), preferred_element_type=F32) * scale_e
        xw = (x_g * scale_w).astype(BF16)
        b_t = jnp.transpose(b_g.astype(F32)).astype(BF16)
        s_ref[g] = s_g * scale_t + jnp.dot(b_t, xw, preferred_element_type=F32)

        cols = slice(g * GROUP_W, (g + 1) * GROUP_W)
        if final:
            y = y + yb_ref[:, cols] + x_g * dsk_ref[:, cols]
            gz = y * _silu(z_ref[:, cols])
            ms = jnp.mean(gz * gz, axis=-1, keepdims=True)
            o_ref[:, cols] = (gz * lax.rsqrt(ms + LN_EPS) * nw_ref[:, cols]).astype(o_ref.dtype)
        else:
            o_ref[:, cols] = y


def _ssd_pass(geom, x, bc, dtr, dtt, dt_bias, a_log, rev, final_args=None):
    ncl = geom.seq // CHUNK
    ncc = geom.ctx // CHUNK
    lat_chunks = geom.n_lat // CHUNK

    def chunk(b, s):
        if rev:
            return jnp.where(s < ncc, lat_chunks + b * ncc + (ncc - 1 - s), b * ncl + (ncl - 1 - (s - ncc)))
        return jnp.where(s < ncc, lat_chunks + b * ncc + s, b * ncl + (s - ncc))

    def rows(width):
        return pl.BlockSpec((CHUNK, width), lambda b, s: (chunk(b, s), 0))

    d_idx = 1 if rev else 0
    small = lambda shape: pl.BlockSpec(shape, lambda b, s: (0, 0))
    in_specs = [rows(D_SSM), rows(D_SSM), rows(LANES),
                pl.BlockSpec((SSM_HEADS, CHUNK), lambda b, s: (d_idx, chunk(b, s))),
                small((1, LANES)), small((SSM_HEADS, 1)), small((1, LANES)), small((SSM_HEADS, 1))]
    lane_pad = (d_idx * SSM_HEADS, LANES - (d_idx + 1) * SSM_HEADS)
    args = [x, bc, dtr, dtt, jnp.pad(dt_bias, lane_pad).reshape(1, LANES), dt_bias.reshape(SSM_HEADS, 1),
            jnp.pad(a_log, lane_pad).reshape(1, LANES), a_log.reshape(SSM_HEADS, 1)]
    final = final_args is not None
    if final:
        y_other, proj, dsk, nw = final_args
        in_specs += [rows(D_SSM), rows(D_SSM), small((1, D_SSM)), small((1, D_SSM))]
        args += [y_other, proj, dsk, nw]
    return pl.pallas_call(
        functools.partial(_ssd_kernel, rev=rev, final=final),
        grid=(geom.bsz, ncc + ncl),
        in_specs=in_specs,
        out_specs=rows(D_SSM),
        out_shape=jax.ShapeDtypeStruct((geom.n_all, D_SSM), BF16 if final else F32),
        scratch_shapes=[pltpu.VMEM((SSM_GROUPS, SSM_STATE, GROUP_W), F32)],
        compiler_params=_cparams(("arbitrary", "arbitrary")),
        name="ssd_fwd" if final else "ssd_bwd",
    )(*args)


CONF_LANE_BLK = 512
CONF_SLOT = 2048


def _shift_matrix(run_len):
    l = jnp.arange(run_len)[:, None, None]
    k = jnp.arange(CONF_WIDTH)[None, :, None]
    j = jnp.arange(run_len)[None, None, :]
    return (j == l + k - CONF_HALF).astype(BF16).reshape(run_len, CONF_WIDTH * run_len)


def _conf_rows_kernel(val_ref, gate_ref, s_run_ref, s_ctx_ref, w_ref, cb_ref, g_ref, b_ref, o_ref, rhs_ref, res_ref,
                      *, lat_tiles):
    i = pl.program_id(0)

    def run_conv(run_len, n_runs, s_ref, slots):
        span = CONF_WIDTH * run_len
        it = 0
        for c0 in range(0, D_CONF, CONF_LANE_BLK):
            lanes = slice(c0, c0 + CONF_LANE_BLK)
            for r in range(n_runs):
                rows = slice(r * run_len, (r + 1) * run_len)
                base = (it % slots) * CONF_SLOT
                it += 1
                glu = (val_ref[rows, lanes] * _sigmoid(gate_ref[rows, lanes])).astype(BF16)
                for k in range(CONF_WIDTH):
                    rhs_ref[base + k * run_len:base + (k + 1) * run_len, :] = (
                        glu * w_ref[k:k + 1, lanes].astype(BF16))
                res_ref[rows, lanes] = (jnp.dot(s_ref[...], rhs_ref[base:base + span, :],
                                                preferred_element_type=F32) + cb_ref[:, lanes])

    @pl.when(i < lat_tiles)
    def _():
        run_conv(GRID_W, ROW_TILE // GRID_W, s_run_ref, 2)

    @pl.when(i >= lat_tiles)
    def _():
        run_conv(ROW_TILE, 1, s_ctx_ref, 1)

    rows_blk = 64
    for r0 in range(0, ROW_TILE, rows_blk):
        v = _standardize(res_ref[r0:r0 + rows_blk, :]) * g_ref[...] + b_ref[...]
        o_ref[r0:r0 + rows_blk, :] = _silu(v).astype(o_ref.dtype)


def _conformer_rows(geom, proj, w, cb, ln_g, ln_b):
    assert geom.ctx == ROW_TILE and CONF_SLOT >= CONF_WIDTH * GRID_W
    n_tiles = geom.n_all // ROW_TILE
    full = lambda shape: pl.BlockSpec(shape, lambda i: (0,) * len(shape))
    return pl.pallas_call(
        functools.partial(_conf_rows_kernel, lat_tiles=geom.lat_tiles),
        grid=(n_tiles,),
        in_specs=[pl.BlockSpec((ROW_TILE, D_CONF), lambda i: (i, COL_VAL)),
                  pl.BlockSpec((ROW_TILE, D_CONF), lambda i: (i, COL_GATE)),
                  full((GRID_W, CONF_WIDTH * GRID_W)), full((ROW_TILE, CONF_WIDTH * ROW_TILE)),
                  full((CONF_WIDTH, D_CONF)), full((1, D_CONF)), full((1, D_CONF)), full((1, D_CONF))],
        out_specs=pl.BlockSpec((ROW_TILE, D_CONF), lambda i: (i, 0)),
        out_shape=jax.ShapeDtypeStruct((geom.n_all, D_CONF), BF16),
        scratch_shapes=[pltpu.VMEM((CONF_WIDTH * ROW_TILE, CONF_LANE_BLK), BF16),
                        pltpu.VMEM((ROW_TILE, D_CONF), F32)],
        compiler_params=_cparams(("arbitrary",)),
        name="conformer_rows",
    )(proj, proj, _shift_matrix(GRID_W), _shift_matrix(ROW_TILE), w, cb.reshape(1, D_CONF),
      ln_g.reshape(1, D_CONF), ln_b.reshape(1, D_CONF))


def _conf_cols_kernel(val_ref, gate_ref, w_ref, cb_ref, o_ref, scr_ref, *, seq):
    halo = CONF_HALF * GRID_W
    scr_ref[0:halo, :] = jnp.zeros((halo, LANES), F32)
    scr_ref[halo + seq:2 * halo + seq, :] = jnp.zeros((halo, LANES), F32)
    blk = 256
    for r0 in range(0, seq, blk):
        scr_ref[halo + r0:halo + r0 + blk, :] = val_ref[r0:r0 + blk, :] * _sigmoid(gate_ref[r0:r0 + blk, :])
    rows_blk = 128

    def body(rb, carry):
        acc = jnp.broadcast_to(cb_ref[...], (rows_blk, LANES))
        for k in range(CONF_WIDTH):
            start = pl.multiple_of(rb * rows_blk + k * GRID_W, GRID_W)
            acc = acc + scr_ref[pl.ds(start, rows_blk), :] * w_ref[k:k + 1, :]
        o_ref[pl.ds(pl.multiple_of(rb * rows_blk, rows_blk), rows_blk), :] = acc
        return carry

    lax.fori_loop(0, seq // rows_blk, body, 0)


def _conformer_cols_conv(geom, proj, w, cb):
    seq = geom.seq
    nblk = D_CONF // LANES
    val0 = COL_VAL * D_CONF // LANES
    gate0 = COL_GATE * D_CONF // LANES
    return pl.pallas_call(
        functools.partial(_conf_cols_kernel, seq=seq),
        grid=(geom.bsz, nblk),
        in_specs=[pl.BlockSpec((seq, LANES), lambda b, c: (b, val0 + c)),
                  pl.BlockSpec((seq, LANES), lambda b, c: (b, gate0 + c)),
                  pl.BlockSpec((CONF_WIDTH, LANES), lambda b, c: (0, c)),
                  pl.BlockSpec((1, LANES), lambda b, c: (0, c))],
        out_specs=pl.BlockSpec((seq, LANES), lambda b, c: (b, c)),
        out_shape=jax.ShapeDtypeStruct((geom.n_lat, D_CONF), F32),
        scratch_shapes=[pltpu.VMEM((seq + 2 * CONF_HALF * GRID_W, LANES), F32)],
        compiler_params=_cparams(("arbitrary", "arbitrary")),
        name="conformer_cols",
    )(proj, proj, w, cb.reshape(1, D_CONF))


def _ln_swish_kernel(v_ref, g_ref, b_ref, o_ref):
    o_ref[...] = _silu(_standardize(v_ref[...]) * g_ref[...] + b_ref[...]).astype(o_ref.dtype)


def _ln_swish(conv, ln_g, ln_b):
    rows = conv.shape[0]
    full = pl.BlockSpec((1, D_CONF), lambda i: (0, 0))
    return pl.pallas_call(
        _ln_swish_kernel,
        grid=(rows // ROW_TILE,),
        in_specs=[pl.BlockSpec((ROW_TILE, D_CONF), lambda i: (i, 0)), full, full],
        out_specs=pl.BlockSpec((ROW_TILE, D_CONF), lambda i: (i, 0)),
        out_shape=jax.ShapeDtypeStruct((rows, D_CONF), BF16),
        compiler_params=_cparams(("arbitrary",)),
        name="conformer_ln",
    )(conv, ln_g.reshape(1, D_CONF), ln_b.reshape(1, D_CONF))


def _top2(logits):
    lane_i = lax.broadcasted_iota(jnp.int32, logits.shape, 1)
    lane = lane_i.astype(F32)
    m1 = jnp.max(logits, axis=-1, keepdims=True)
    i1 = jnp.min(jnp.where(logits == m1, lane, float(LANES)), axis=-1, keepdims=True)
    rest = jnp.where(lane == i1, -jnp.inf, logits)
    m2 = jnp.max(rest, axis=-1, keepdims=True)
    i2 = jnp.min(jnp.where(rest == m2, lane, float(LANES)), axis=-1, keepdims=True)
    e2 = jnp.exp(m2 - m1)
    w1 = 1.0 / (1.0 + e2)
    w2 = e2 / (1.0 + e2)
    idx = jnp.where(lane_i == 0, i1, jnp.where(lane_i == 1, i2, 0.0)).astype(jnp.int32)
    wts = jnp.where(lane_i == 0, w1, jnp.where(lane_i == 1, w2, 0.0))
    return idx, wts


HALF_D = D_MODEL // 2
_HI_MASK = 0xFFFF0000


def _pack_bf16_pairs(u):
    bits = pltpu.bitcast(u.astype(BF16).astype(F32), jnp.uint32)
    return (bits[:, HALF_D:] & jnp.uint32(_HI_MASK)) | (bits[:, :HALF_D] >> 16)


def _unpack_bf16_pairs(p):
    lo = pltpu.bitcast(p << 16, F32).astype(BF16)
    hi = pltpu.bitcast(p & jnp.uint32(_HI_MASK), F32).astype(BF16)
    return lo, hi


def _residual_norm(h, f, gate, ln_g, ln_b):
    return _standardize(ALPHA * h + gate * f) * ln_g + ln_b


def _epilogue_kernel(*refs, modulate, router, lat_tiles):
    if lat_tiles is not None:
        h_in = jnp.where(pl.program_id(0) < lat_tiles, refs[0][...], refs[1][...])
        refs = refs[1:]
    else:
        h_in = refs[0][...]
    _, f_ref, gate_ref, lg_ref, lb_ref = refs[:5]
    pos = 5
    if modulate:
        sh_ref, sc_ref = refs[pos:pos + 2]
        pos += 2
    if router:
        rwh_ref, rwl_ref, rb_ref = refs[pos:pos + 3]
        pos += 3
    outs = refs[pos:]
    hn = _residual_norm(h_in, f_ref[...], gate_ref[...], lg_ref[...], lb_ref[...])
    outs[0][...] = hn
    if modulate:
        u = _standardize(hn) * (1.0 + sc_ref[...]) + sh_ref[...]
        if router:
            outs[1][...] = _pack_bf16_pairs(u)
        else:
            outs[1][...] = u.astype(BF16)
    if router:
        u_hi = u.astype(BF16)
        u_lo = (u - u_hi.astype(F32)).astype(BF16)
        logits = (jnp.dot(u_hi, rwh_ref[...], preferred_element_type=F32)
                  + jnp.dot(u_lo, rwh_ref[...], preferred_element_type=F32)
                  + jnp.dot(u_hi, rwl_ref[...], preferred_element_type=F32)) + rb_ref[...]
        lane = lax.broadcasted_iota(jnp.int32, logits.shape, 1)
        logits = jnp.where(lane < N_EXPERTS, logits, -jnp.inf)
        idx, wts = _top2(logits)
        outs[2][...] = idx
        outs[3][...] = wts


def _epilogue(geom, h, f, rows, mods, gate_slot, ln_g, ln_b, next_mods=None, next_slots=None, router=None):
    n_tiles = rows // ROW_TILE
    row = pl.BlockSpec((ROW_TILE, D_MODEL), lambda i: (i, 0))
    vec = pl.BlockSpec((1, D_MODEL), lambda i: (0, 0))
    lat_tiles = None
    if isinstance(h, tuple):
        lat_tiles = geom.lat_tiles
        h_specs, h_args = _split_row_specs(lat_tiles), list(h)
    else:
        h_specs, h_args = [row], [h]
    in_specs = h_specs + [row, _mod_spec(geom, gate_slot), vec, vec]
    args = h_args + [f, mods, ln_g.reshape(1, D_MODEL), ln_b.reshape(1, D_MODEL)]
    out_specs = [row]
    out_shape = [jax.ShapeDtypeStruct((rows, D_MODEL), F32)]
    modulate = next_mods is not None
    if modulate:
        in_specs += [_mod_spec(geom, next_slots[0]), _mod_spec(geom, next_slots[1])]
        args += [next_mods, next_mods]
        if router is not None:
            out_specs.append(pl.BlockSpec((ROW_TILE, HALF_D), lambda i: (i, 0)))
            out_shape.append(jax.ShapeDtypeStruct((rows, HALF_D), jnp.uint32))
        else:
            out_specs.append(row)
            out_shape.append(jax.ShapeDtypeStruct((rows, D_MODEL), BF16))
    if router is not None:
        rw_hi, rw_lo, rb = router
        small = pl.BlockSpec((D_MODEL, LANES), lambda i: (0, 0))
        in_specs += [small, small, pl.BlockSpec((1, LANES), lambda i: (0, 0))]
        args += [rw_hi, rw_lo, rb]
        lane_blk = pl.BlockSpec((ROW_TILE, LANES), lambda i: (i, 0))
        out_specs += [lane_blk, lane_blk]
        out_shape += [jax.ShapeDtypeStruct((rows, LANES), jnp.int32),
                      jax.ShapeDtypeStruct((rows, LANES), F32)]
    return pl.pallas_call(
        functools.partial(_epilogue_kernel, modulate=modulate, router=router is not None, lat_tiles=lat_tiles),
        grid=(n_tiles,),
        in_specs=in_specs,
        out_specs=out_specs,
        out_shape=out_shape,
        compiler_params=_cparams(("arbitrary",)),
        name="epilogue",
    )(*args)


GATHER_ROWS = 256


def _row_copy(src_hbm, dst_vmem, sem, src_row, dst_row):
    return pltpu.make_async_copy(src_hbm.at[pl.ds(src_row, 1), :], dst_vmem.at[pl.ds(dst_row, 1), :], sem)


def _issue_rows(src_ref, dst_refs, sem, idx_fn, n_rows):
    def body(r, carry):
        for p, dst in enumerate(dst_refs):
            _row_copy(src_ref, dst, sem, idx_fn(len(dst_refs) * r + p), r).start(priority=p)
        return carry

    lax.fori_loop(0, n_rows, body, 0, unroll=8)


def _drain_rows(src_ref, dst_refs, sem, n_rows):
    for dst in dst_refs:
        pltpu.make_async_copy(src_ref.at[pl.ds(0, n_rows), :], dst, sem).wait()


def _double_buffered_rows(src_ref, bufs_of_slot, sem, cur_idx_ref, nxt_idx_ref, n_rows):
    i = pl.program_id(0)
    slot = i % 2

    @pl.when(i == 0)
    def _():
        _issue_rows(src_ref, bufs_of_slot(0), sem.at[0], lambda q: cur_idx_ref[0, 0, q], n_rows)

    @pl.when(i + 1 < pl.num_programs(0))
    def _():
        _issue_rows(src_ref, bufs_of_slot(1 - slot), sem.at[1 - slot], lambda q: nxt_idx_ref[0, 0, q], n_rows)

    _drain_rows(src_ref, bufs_of_slot(slot), sem.at[slot], n_rows)
    return slot


def _idx_specs(per_step, steps):
    cur = pl.BlockSpec((1, 1, per_step), lambda i: (i, 0, 0), memory_space=pltpu.SMEM)
    nxt = pl.BlockSpec((1, 1, per_step), lambda i: (jnp.minimum(i + 1, steps - 1), 0, 0), memory_space=pltpu.SMEM)
    return cur, nxt


def _gather_kernel(idx_ref, nxt_ref, src_ref, o_ref, buf_ref, sem):
    half = GATHER_ROWS // 2

    def bufs(slot):
        return (buf_ref.at[slot, 0], buf_ref.at[slot, 1])

    slot = _double_buffered_rows(src_ref, bufs, sem, idx_ref, nxt_ref, half)
    for p in range(2):
        lo, hi = _unpack_bf16_pairs(buf_ref[slot, p])
        o_ref[p, :, :HALF_D] = lo
        o_ref[p, :, HALF_D:] = hi


def _gather_rows(src, slot_token):
    n_slots = slot_token.shape[0]
    width = src.shape[1]
    steps = n_slots // GATHER_ROWS
    half = GATHER_ROWS // 2
    idx = jnp.transpose(slot_token.reshape(steps, 2, half), (0, 2, 1)).reshape(steps, 1, GATHER_ROWS)
    cur, nxt = _idx_specs(GATHER_ROWS, steps)
    out = pl.pallas_call(
        _gather_kernel,
        grid=(steps,),
        in_specs=[cur, nxt, pl.BlockSpec(memory_space=pl.ANY)],
        out_specs=pl.BlockSpec((None, 2, half, 2 * width), lambda i: (i, 0, 0, 0)),
        out_shape=jax.ShapeDtypeStruct((steps, 2, half, 2 * width), BF16),
        scratch_shapes=[pltpu.VMEM((2, 2, half, width), src.dtype), pltpu.SemaphoreType.DMA((2,))],
        compiler_params=_cparams(("arbitrary",)),
        name="moe_gather",
    )(idx, idx, src)
    return out.reshape(n_slots, 2 * width)


def _tile_idx(w, nused_ref):
    return jnp.minimum(w, nused_ref[0] - 1)


def _fresh_expert(te_ref, w):
    return (w == 0) | (te_ref[w] != te_ref[jnp.maximum(w - 1, 0)])


def _moe_up_kernel(te_ref, nused_ref, a_ref, w1_ref, w3_ref, o_ref, w1b_ref, w3b_ref):
    w = pl.program_id(1)
    used = w < nused_ref[0]

    @pl.when(used & _fresh_expert(te_ref, w))
    def _():
        w1b_ref[...] = w1_ref[...].astype(BF16)
        w3b_ref[...] = w3_ref[...].astype(BF16)

    @pl.when(used)
    def _():
        a = a_ref[...]
        h1 = jnp.dot(a, w1b_ref[...], preferred_element_type=F32)
        h3 = jnp.dot(a, w3b_ref[...], preferred_element_type=F32)
        o_ref[...] = (_silu(h1) * h3).astype(o_ref.dtype)

    @pl.when(jnp.logical_not(used))
    def _():
        o_ref[...] = jnp.zeros_like(o_ref)


MOE_UP_VMEM = 59 * 1024 * 1024


def _moe_up(a_sorted, w1, w3, tile_expert, n_used, tf=512):
    n_slots, k = a_sorted.shape
    n_tiles = n_slots // MOE_TM
    f = w1.shape[2]
    w_map = lambda j, w, te, nu: (te[_tile_idx(w, nu)], 0, j)
    grid_spec = pltpu.PrefetchScalarGridSpec(
        num_scalar_prefetch=2,
        grid=(f // tf, n_tiles),
        in_specs=[pl.BlockSpec((MOE_TM, k), lambda j, w, te, nu: (_tile_idx(w, nu), 0)),
                  pl.BlockSpec((None, k, tf), w_map),
                  pl.BlockSpec((None, k, tf), w_map)],
        out_specs=pl.BlockSpec((MOE_TM, tf), lambda j, w, te, nu: (w, j)),
        scratch_shapes=[pltpu.VMEM((k, tf), BF16), pltpu.VMEM((k, tf), BF16)],
    )
    return pl.pallas_call(
        _moe_up_kernel,
        grid_spec=grid_spec,
        out_shape=jax.ShapeDtypeStruct((n_slots, f), BF16),
        compiler_params=_cparams(("arbitrary", "arbitrary"), MOE_UP_VMEM),
        name="moe_up",
    )(tile_expert, n_used, a_sorted, w1, w3)


def _moe_down_kernel(te_ref, nused_ref, a_ref, w_ref, o_ref, wb_ref):
    w = pl.program_id(1)
    used = w < nused_ref[0]

    @pl.when(used & _fresh_expert(te_ref, w))
    def _():
        wb_ref[...] = w_ref[...].astype(BF16)

    @pl.when(used)
    def _():
        o_ref[...] = jnp.dot(a_ref[...], wb_ref[...], preferred_element_type=F32)

    @pl.when(jnp.logical_not(used))
    def _():
        o_ref[...] = jnp.zeros_like(o_ref)


def _moe_down(hid, w2, tile_expert, n_used, tn=1024):
    n_slots, k = hid.shape
    n_tiles = n_slots // MOE_TM
    n = w2.shape[2]
    grid_spec = pltpu.PrefetchScalarGridSpec(
        num_scalar_prefetch=2,
        grid=(n // tn, n_tiles),
        in_specs=[pl.BlockSpec((MOE_TM, k), lambda j, w, te, nu: (_tile_idx(w, nu), 0)),
                  pl.BlockSpec((None, k, tn), lambda j, w, te, nu: (te[_tile_idx(w, nu)], 0, j))],
        out_specs=pl.BlockSpec((MOE_TM, tn), lambda j, w, te, nu: (w, j)),
        scratch_shapes=[pltpu.VMEM((k, tn), BF16)],
    )
    return pl.pallas_call(
        _moe_down_kernel,
        grid_spec=grid_spec,
        out_shape=jax.ShapeDtypeStruct((n_slots, n), F32),
        compiler_params=_cparams(("arbitrary", "arbitrary")),
        name="moe_down",
    )(tile_expert, n_used, hid, w2)


COMBINE_ROWS = 128


def _combine_kernel(pos_ref, nxt_ref, h_ref, wts_ref, gate_ref, lg_ref, lb_ref, y_ref, o_ref, buf_ref, sem):
    def bufs(slot):
        return (buf_ref.at[slot, 0], buf_ref.at[slot, 1])

    slot = _double_buffered_rows(y_ref, bufs, sem, pos_ref, nxt_ref, COMBINE_ROWS)
    wts = wts_ref[...]
    f = buf_ref[slot, 0] * wts[:, 0:1] + buf_ref[slot, 1] * wts[:, 1:2]
    o_ref[...] = _residual_norm(h_ref[...], f, gate_ref[...], lg_ref[...], lb_ref[...])


def _moe_combine(geom, h, y_sorted, pos, wts, mods, gate_slot, ln_g, ln_b):
    rows = geom.n_lat
    steps = rows // COMBINE_ROWS
    per_mod = ROW_TILE // COMBINE_ROWS
    row = pl.BlockSpec((COMBINE_ROWS, D_MODEL), lambda i: (i, 0))
    vec = pl.BlockSpec((1, D_MODEL), lambda i: (0, 0))
    cur, nxt = _idx_specs(2 * COMBINE_ROWS, steps)
    pos3 = pos.reshape(steps, 1, 2 * COMBINE_ROWS)
    return pl.pallas_call(
        _combine_kernel,
        grid=(steps,),
        in_specs=[cur, nxt, row,
                  pl.BlockSpec((COMBINE_ROWS, LANES), lambda i: (i, 0)),
                  pl.BlockSpec((None, 1, D_MODEL), lambda i: (geom.mod_row(i // per_mod) * 6 + gate_slot, 0, 0)),
                  vec, vec,
                  pl.BlockSpec(memory_space=pl.ANY)],
        out_specs=row,
        out_shape=jax.ShapeDtypeStruct((rows, D_MODEL), F32),
        scratch_shapes=[pltpu.VMEM((2, 2, COMBINE_ROWS, D_MODEL), F32), pltpu.SemaphoreType.DMA((2,))],
        compiler_params=_cparams(("arbitrary",)),
        name="moe_combine",
    )(pos3, pos3, h, wts, mods, ln_g.reshape(1, D_MODEL), ln_b.reshape(1, D_MODEL), y_sorted)


def _route(top_idx, n_tokens):
    flat = top_idx.reshape(-1)
    onehot = (jnp.arange(N_EXPERTS, dtype=jnp.int32)[:, None] == flat[None, :]).astype(jnp.int32)
    blk = 2 * ROW_TILE
    inner = jnp.cumsum(onehot.reshape(N_EXPERTS, -1, blk), axis=2)
    blk_tot = inner[:, :, -1]
    blk_off = jnp.cumsum(blk_tot, axis=1) - blk_tot
    csum = (inner + blk_off[:, :, None]).reshape(N_EXPERTS, -1)
    rank = jnp.sum(csum * onehot, axis=0) - 1
    counts = csum[:, -1]
    tiles_per = (counts + MOE_TM - 1) // MOE_TM
    tile_end = jnp.cumsum(tiles_per)
    tile_start = tile_end - tiles_per
    pos = (jnp.sum(onehot * (tile_start * MOE_TM)[:, None], axis=0) + rank).astype(jnp.int32)
    n_tiles = (2 * n_tokens) // MOE_TM + N_EXPERTS
    n_used = tile_end[-1].astype(jnp.int32)
    tile_ids = jnp.arange(n_tiles, dtype=jnp.int32)
    tile_expert = jnp.sum((tile_ids[:, None] >= tile_end[None, :]).astype(jnp.int32), axis=1)
    tile_expert = jnp.minimum(tile_expert, N_EXPERTS - 1).astype(jnp.int32)
    slot_token = jnp.zeros((n_tiles * MOE_TM,), jnp.int32).at[pos].set(
        jnp.arange(2 * n_tokens, dtype=jnp.int32) // 2)
    return pos, slot_token, tile_expert, n_used.reshape(1)


def kernel(x, c, ctx, c_ctx, ada_w, ada_b, w_in, mamba_conv_w, mamba_conv_b, dt_bias_fwd, dt_bias_bwd, a_log_fwd, a_log_bwd, d_skip, ssm_norm_w, conf_conv_w, conf_conv_b, conf_ln_g, conf_ln_b, w_out, ln1_g, ln1_b, ln2_g, ln2_b, ffn_w1, ffn_w3, ffn_w2, router_w, router_b, moe_w1, moe_w3, moe_w2):
    bsz, seq, d = x.shape
    ctx_len = ctx.shape[1]
    geom = _Geom(bsz, seq, ctx_len)
    n_all, n_lat = geom.n_all, geom.n_lat

    mod_rows = SUBLANES
    cond_rows = jnp.zeros((mod_rows, d), F32).at[:bsz].set(c).at[bsz].set(c_ctx)
    mods_all = _ada_mods(cond_rows, ada_w, ada_b).reshape(DEPTH, mod_rows * 6, 1, d)

    h = (x.reshape(n_lat, d), ctx.reshape(bsz * ctx_len, d))
    u = _pack_modulate(geom, h[0], h[1], mods_all[0])

    tm_all = _pick_tile(n_all, 1056)
    tm_lat = _pick_tile(n_lat, 1024)

    w_main_all, w_dt_all = _w_in_prep(w_in)
    w_out_all = w_out.astype(BF16)

    for i in range(DEPTH):
        mods = mods_all[i]
        last = i == DEPTH - 1
        proj = _matmul([(u, w_main_all, (i, 0))], n_all, tm_all, 1024, F32, "in_proj")
        dtr = _matmul([(u, w_dt_all, (i, 0))], n_all, tm_all, LANES, F32, "dt_proj")
        dtt = jnp.transpose(dtr[:, :2 * SSM_HEADS])

        xs, bc = _ssm_conv(geom, proj, mamba_conv_w[i], mamba_conv_b[i])
        y_b = _ssd_pass(geom, xs, bc, dtr, dtt, dt_bias_bwd[i], a_log_bwd[i], rev=True)
        dsk = jnp.repeat(d_skip[i], SSM_HEADDIM).reshape(1, D_SSM)
        y_ssm = _ssd_pass(geom, xs, bc, dtr, dtt, dt_bias_fwd[i], a_log_fwd[i], rev=False,
                          final_args=(y_b, proj, dsk, ssm_norm_w[i].reshape(1, D_SSM)))

        if i % 2 == 0:
            v = _conformer_rows(geom, proj, conf_conv_w[i], conf_conv_b[i], conf_ln_g[i], conf_ln_b[i])
        else:
            conv = _conformer_cols_conv(geom, proj, conf_conv_w[i], conf_conv_b[i])
            v = _ln_swish(conv, conf_ln_g[i], conf_ln_b[i])

        rows = n_lat if last else n_all
        tm = tm_lat if last else tm_all
        mix = _matmul([(y_ssm, w_out_all, (i, 0)), (v, w_out_all, (i, 1))], rows, tm, 1024, F32, "out_proj")

        j = i // 2
        if i % 2 == 0:
            h, u2 = _epilogue(geom, h, mix, rows, mods, 2, ln1_g[i], ln1_b[i], mods, (3, 4))
            w2 = ffn_w2[j].astype(BF16)
            hid = _gated_up(u2, ffn_w1[j], ffn_w3[j], rows, _pick_tile(rows, 1408), 256, FFN_PAD)
            f = _matmul_ktiled(hid, w2, rows, tm, 2048, 1024, "ffn_down")
            if last:
                h = _epilogue(geom, h, f, rows, mods, 5, ln2_g[i], ln2_b[i])[0]
            else:
                h, u = _epilogue(geom, h, f, rows, mods, 5, ln2_g[i], ln2_b[i], mods_all[i + 1], (0, 1))
        else:
            rw = jnp.pad(router_w[j], ((0, 0), (0, LANES - N_EXPERTS)))
            rw_hi = rw.astype(BF16)
            rw_lo = (rw - rw_hi.astype(F32)).astype(BF16)
            rb = jnp.pad(router_b[j], (0, LANES - N_EXPERTS)).reshape(1, LANES)
            assert last, "a routed layer is only supported as the final layer"
            h, u2p, top_idx, top_w = _epilogue(geom, h, mix, rows, mods, 2, ln1_g[i], ln1_b[i], mods, (3, 4),
                                                  router=(rw_hi, rw_lo, rb))
            pos, slot_token, tile_expert, n_used = _route(top_idx[:, :2], rows)
            a_sorted = _gather_rows(u2p, slot_token)
            hid = _moe_up(a_sorted, moe_w1[j], moe_w3[j], tile_expert, n_used)
            y_sorted = _moe_down(hid, moe_w2[j], tile_expert, n_used)
            h = _moe_combine(geom, h, y_sorted, pos, top_w, mods, 5, ln2_g[i], ln2_b[i])

    return h[:n_lat].reshape(bsz, seq, d)
```

```python
import functools

import jax
import jax.numpy as jnp
from jax import lax
from jax.experimental import pallas as pl
from jax.experimental.pallas import tpu as pltpu

F32 = jnp.float32
BF16 = jnp.bfloat16

D_MODEL = 4096
DEPTH = 2
GRID_W = 64
D_SSM = 2048
D_CONF = 2048
SSM_HEADDIM = 64
SSM_HEADS = 32
SSM_GROUPS = 8
HEADS_PER_GROUP = 4
SSM_STATE = 128
SSM_CONV = 5
CHUNK = 128
GROUP_W = HEADS_PER_GROUP * SSM_HEADDIM
CONF_WIDTH = 31
CONF_HALF = CONF_WIDTH // 2
OFF_DT = D_SSM
OFF_X = OFF_DT + 2 * SSM_HEADS
N_MAIN = 10240
COL_X, COL_BC, COL_VAL, COL_GATE = 1, 2, 3, 4
FFN_DENSE = 11008
FFN_PAD = 11264
N_EXPERTS = 8
FFN_EXPERT = 3584
LN_EPS = 1e-5
ALPHA = (2 * DEPTH) ** 0.25

LANES = 128
SUBLANES = 8
ROW_TILE = 256
MOE_TM = 512
VMEM_LIMIT = 56 * 1024 * 1024


def _cparams(sem, vmem=VMEM_LIMIT):
    return pltpu.CompilerParams(dimension_semantics=sem, vmem_limit_bytes=vmem)


def _sigmoid(x):
    return 1.0 / (1.0 + jnp.exp(-x))


def _silu(x):
    return x * _sigmoid(x)


def _softplus(x):
    return jnp.maximum(x, 0.0) + jnp.log(1.0 + jnp.exp(-jnp.abs(x)))


def _standardize(x):
    mu = jnp.mean(x, axis=-1, keepdims=True)
    xc = x - mu
    var = jnp.mean(xc * xc, axis=-1, keepdims=True)
    return xc * lax.rsqrt(var + LN_EPS)


def _pick_tile(n, target, mult=16):
    best = None
    for t in range(mult, min(n, target) + 1, mult):
        if n % t == 0:
            best = t
    assert best is not None, (n, target)
    return best


def _ada_kernel(c_ref, w_ref, b_ref, o_ref):
    cond = _silu(c_ref[...]).astype(BF16)
    o_ref[...] = jnp.dot(cond, w_ref[...].astype(BF16), preferred_element_type=F32) + b_ref[...]


def _ada_mods(cond_rows, ada_w, ada_b):
    depth, d, n = ada_w.shape
    rows = cond_rows.shape[0]
    tn = 1024
    return pl.pallas_call(
        _ada_kernel,
        grid=(depth, n // tn),
        in_specs=[
            pl.BlockSpec((rows, d), lambda l, j: (0, 0)),
            pl.BlockSpec((None, d, tn), lambda l, j: (l, 0, j)),
            pl.BlockSpec((None, 1, tn), lambda l, j: (l, 0, j)),
        ],
        out_specs=pl.BlockSpec((None, rows, tn), lambda l, j: (l, 0, j)),
        out_shape=jax.ShapeDtypeStruct((depth, rows, n), F32),
        compiler_params=_cparams(("arbitrary", "arbitrary")),
        name="ada_mods",
    )(cond_rows, ada_w, ada_b.reshape(depth, 1, n))


class _Geom:
    def __init__(self, bsz, seq, ctx_len):
        self.bsz, self.seq, self.ctx = bsz, seq, ctx_len
        self.n_lat = bsz * seq
        self.n_all = self.n_lat + bsz * ctx_len
        assert seq % ROW_TILE == 0 and ctx_len % ROW_TILE == 0 and seq % GRID_W == 0
        self.lat_tiles = self.n_lat // ROW_TILE
        self.tiles_per_seq = seq // ROW_TILE
        self.tiles_per_ctx = ctx_len // ROW_TILE

    def mod_row(self, i):
        return jnp.where(i < self.lat_tiles, i // self.tiles_per_seq, self.bsz)

    def seq_edges(self, i):
        lat = i < self.lat_tiles
        j = i - self.lat_tiles
        first = jnp.where(lat, i % self.tiles_per_seq == 0, j % self.tiles_per_ctx == 0)
        last = jnp.where(lat, i % self.tiles_per_seq == self.tiles_per_seq - 1,
                         j % self.tiles_per_ctx == self.tiles_per_ctx - 1)
        return first, last


def _mod_spec(geom, slot):
    return pl.BlockSpec((None, 1, D_MODEL), lambda i: (geom.mod_row(i) * 6 + slot, 0, 0))


def _split_row_specs(lat_tiles):
    return [pl.BlockSpec((ROW_TILE, D_MODEL), lambda i: (jnp.minimum(i, lat_tiles - 1), 0)),
            pl.BlockSpec((ROW_TILE, D_MODEL), lambda i: (jnp.maximum(i - lat_tiles, 0), 0))]


def _pack_mod_kernel(x_ref, c_ref, sh_ref, sc_ref, u_ref, *, lat_tiles):
    v = jnp.where(pl.program_id(0) < lat_tiles, x_ref[...], c_ref[...])
    u_ref[...] = (_standardize(v) * (1.0 + sc_ref[...]) + sh_ref[...]).astype(BF16)


def _pack_modulate(geom, x2, ctx2, mods):
    n_tiles = geom.n_all // ROW_TILE
    return pl.pallas_call(
        functools.partial(_pack_mod_kernel, lat_tiles=geom.lat_tiles),
        grid=(n_tiles,),
        in_specs=_split_row_specs(geom.lat_tiles) + [_mod_spec(geom, 0), _mod_spec(geom, 1)],
        out_specs=pl.BlockSpec((ROW_TILE, D_MODEL), lambda i: (i, 0)),
        out_shape=jax.ShapeDtypeStruct((geom.n_all, D_MODEL), BF16),
        compiler_params=_cparams(("arbitrary",)),
        name="pack_modulate",
    )(x2, ctx2, mods, mods)


def _mm_kernel(*refs, n_pairs):
    o_ref = refs[-1]
    acc = jnp.dot(refs[0][...], refs[1][...], preferred_element_type=F32)
    for p in range(1, n_pairs):
        acc = acc + jnp.dot(refs[2 * p][...], refs[2 * p + 1][...], preferred_element_type=F32)
    o_ref[...] = acc.astype(o_ref.dtype)


def _matmul(pairs, rows, tm, tn, out_dtype, name):
    n = pairs[0][1].shape[-1]
    assert rows % tm == 0 and n % tn == 0
    in_specs, args = [], []
    for a, w, row_blk in pairs:
        k = a.shape[1]
        if isinstance(row_blk, tuple):
            w_spec = pl.BlockSpec((None, k, tn), lambda i, j, rb=row_blk: (rb[0], rb[1], j))
        else:
            w_spec = pl.BlockSpec((k, tn), lambda i, j, rb=row_blk: (rb, j))
        in_specs += [pl.BlockSpec((tm, k), lambda i, j: (i, 0)), w_spec]
        args += [a, w]
    return pl.pallas_call(
        functools.partial(_mm_kernel, n_pairs=len(pairs)),
        grid=(rows // tm, n // tn),
        in_specs=in_specs,
        out_specs=pl.BlockSpec((tm, tn), lambda i, j: (i, j)),
        out_shape=jax.ShapeDtypeStruct((rows, n), out_dtype),
        compiler_params=_cparams(("arbitrary", "arbitrary")),
        name=name,
    )(*args)


def _gated_up_kernel(a_ref, w1_ref, w3_ref, o_ref, *, valid_tiles):
    a = a_ref[...]
    h1 = jnp.dot(a, w1_ref[...].astype(BF16), preferred_element_type=F32)
    h3 = jnp.dot(a, w3_ref[...].astype(BF16), preferred_element_type=F32)
    keep = pl.program_id(1) < valid_tiles
    o_ref[...] = jnp.where(keep, _silu(h1) * h3, 0.0).astype(o_ref.dtype)


def _gated_up(a, w1, w3, rows, tm, tn, n_out):
    k, n_valid = w1.shape
    n = n_out
    assert n_valid % tn == 0 and n % tn == 0
    valid_tiles = n_valid // tn
    w_spec = pl.BlockSpec((k, tn), lambda i, j: (0, jnp.minimum(j, valid_tiles - 1)))
    return pl.pallas_call(
        functools.partial(_gated_up_kernel, valid_tiles=valid_tiles),
        grid=(rows // tm, n // tn),
        in_specs=[pl.BlockSpec((tm, k), lambda i, j: (i, 0)), w_spec, w_spec],
        out_specs=pl.BlockSpec((tm, tn), lambda i, j: (i, j)),
        out_shape=jax.ShapeDtypeStruct((rows, n), BF16),
        compiler_params=_cparams(("arbitrary", "arbitrary")),
        name="ffn_up",
    )(a, w1, w3)


def _mm_acc_kernel(a_ref, w_ref, o_ref, acc_ref, *, k_valid):
    k = pl.program_id(2)
    last = pl.num_programs(2) - 1
    tk = w_ref.shape[0]

    @pl.when(k == 0)
    def _():
        acc_ref[...] = jnp.zeros_like(acc_ref)

    def accumulate(w):
        acc_ref[...] += jnp.dot(a_ref[...], w, preferred_element_type=F32)

    tail = k_valid % tk
    if tail == 0:
        accumulate(w_ref[...])
    else:
        @pl.when(k < last)
        def _():
            accumulate(w_ref[...])

        @pl.when(k == last)
        def _():
            row = lax.broadcasted_iota(jnp.int32, w_ref.shape, 0)
            accumulate(jnp.where(row < tail, w_ref[...], jnp.zeros_like(w_ref)))

    @pl.when(k == last)
    def _():
        o_ref[...] = acc_ref[...]


def _matmul_ktiled(a, w, rows, tm, tn, tk, name):
    kk = a.shape[1]
    k_valid, n = w.shape
    assert rows % tm == 0 and n % tn == 0 and kk % tk == 0 and kk - k_valid < tk
    return pl.pallas_call(
        functools.partial(_mm_acc_kernel, k_valid=k_valid),
        grid=(rows // tm, n // tn, kk // tk),
        in_specs=[pl.BlockSpec((tm, tk), lambda i, j, k: (i, k)),
                  pl.BlockSpec((tk, tn), lambda i, j, k: (k, j))],
        out_specs=pl.BlockSpec((tm, tn), lambda i, j, k: (i, j)),
        out_shape=jax.ShapeDtypeStruct((rows, n), F32),
        scratch_shapes=[pltpu.VMEM((tm, tn), F32)],
        compiler_params=_cparams(("arbitrary", "arbitrary", "arbitrary")),
        name=name,
    )(a, w)


def _conv5_kernel(xm_ref, xp_ref, xn_ref, bm_ref, bp_ref, bn_ref, w_ref, b_ref, xo_ref, bco_ref, scr_ref,
                  *, geom):
    i = pl.program_id(0)
    first, last = geom.seq_edges(i)
    half = D_SSM
    pad = SUBLANES
    for part, (m_ref, p_ref, n_ref) in enumerate(((xm_ref, xp_ref, xn_ref), (bm_ref, bp_ref, bn_ref))):
        lo = part * half
        scr_ref[0:pad, lo:lo + half] = jnp.where(first, 0.0, p_ref[...])
        scr_ref[pad:pad + ROW_TILE, lo:lo + half] = m_ref[...]
        scr_ref[pad + ROW_TILE:2 * pad + ROW_TILE, lo:lo + half] = jnp.where(last, 0.0, n_ref[...])
    rows_blk, lane_blk = 64, 512
    base = pad - SSM_CONV // 2
    for c0 in range(0, 2 * half, lane_blk):
        wc = w_ref[:, c0:c0 + lane_blk]
        bias = b_ref[:, c0:c0 + lane_blk]
        for r0 in range(0, ROW_TILE, rows_blk):
            acc = bias
            for k in range(SSM_CONV):
                acc = acc + scr_ref[base + r0 + k:base + r0 + k + rows_blk, c0:c0 + lane_blk] * wc[k:k + 1, :]
            val = _silu(acc)
            if c0 < half:
                xo_ref[r0:r0 + rows_blk, c0:c0 + lane_blk] = val
            else:
                bco_ref[r0:r0 + rows_blk, c0 - half:c0 - half + lane_blk] = val.astype(BF16)


def _ssm_conv(geom, proj, conv_w, conv_b):
    n_tiles = geom.n_all // ROW_TILE
    sub = ROW_TILE // SUBLANES
    last_blk = geom.n_all // SUBLANES - 1

    def main(col):
        return pl.BlockSpec((ROW_TILE, D_SSM), lambda i: (i, col))

    def prev(col):
        return pl.BlockSpec((SUBLANES, D_SSM), lambda i: (jnp.maximum(i * sub - 1, 0), col))

    def nxt(col):
        return pl.BlockSpec((SUBLANES, D_SSM), lambda i: (jnp.minimum((i + 1) * sub, last_blk), col))

    width = 2 * D_SSM
    return pl.pallas_call(
        functools.partial(_conv5_kernel, geom=geom),
        grid=(n_tiles,),
        in_specs=[main(COL_X), prev(COL_X), nxt(COL_X), main(COL_BC), prev(COL_BC), nxt(COL_BC),
                  pl.BlockSpec((SSM_CONV, width), lambda i: (0, 0)),
                  pl.BlockSpec((1, width), lambda i: (0, 0))],
        out_specs=[pl.BlockSpec((ROW_TILE, D_SSM), lambda i: (i, 0)),
                   pl.BlockSpec((ROW_TILE, D_SSM), lambda i: (i, 0))],
        out_shape=[jax.ShapeDtypeStruct((geom.n_all, D_SSM), F32),
                   jax.ShapeDtypeStruct((geom.n_all, D_SSM), BF16)],
        scratch_shapes=[pltpu.VMEM((ROW_TILE + 2 * SUBLANES, width), F32)],
        compiler_params=_cparams(("arbitrary",)),
        name="ssm_conv",
    )(proj, proj, proj, proj, proj, proj, conv_w, conv_b.reshape(1, width))


def _split3_dot(lhs_fn, v):
    v1 = v.astype(BF16)
    r1 = v - v1.astype(F32)
    v2 = r1.astype(BF16)
    v3 = (r1 - v2.astype(F32)).astype(BF16)
    return lhs_fn(v1) + lhs_fn(v2) + lhs_fn(v3)


def _ssd_kernel(*refs, rev, final):
    if final:
        (x_ref, bc_ref, dtr_ref, dtt_ref, pbr_ref, pbc_ref, alr_ref, alc_ref,
         yb_ref, z_ref, dsk_ref, nw_ref, o_ref, s_ref) = refs
    else:
        x_ref, bc_ref, dtr_ref, dtt_ref, pbr_ref, pbc_ref, alr_ref, alc_ref, o_ref, s_ref = refs
    step = pl.program_id(1)

    @pl.when(step == 0)
    def _():
        s_ref[...] = jnp.zeros_like(s_ref)

    n = CHUNK
    row = lax.broadcasted_iota(jnp.int32, (n, n), 0)
    col = lax.broadcasted_iota(jnp.int32, (n, n), 1)
    tri = (row <= col) if rev else (row >= col)
    cum_l = tri.astype(BF16)
    cum_r = jnp.logical_not(tri) | (row == col)
    cum_r = cum_r.astype(BF16)
    h_off = SSM_HEADS if rev else 0
    edge = 0 if rev else n - 1

    dt = _softplus(dtr_ref[...] + pbr_ref[...])
    a = -dt * jnp.exp(alr_ref[...])
    dtt = _softplus(dtt_ref[...] + pbc_ref[...])
    at = -dtt * jnp.exp(alc_ref[...])
    acum = _split3_dot(lambda v: jnp.dot(cum_l, v, preferred_element_type=F32), a)
    acum_t = _split3_dot(lambda v: jnp.dot(v, cum_r, preferred_element_type=F32), at)

    lane_lo = lax.broadcasted_iota(jnp.int32, (n, LANES), 1) < SSM_HEADDIM
    lane_w = lax.broadcasted_iota(jnp.int32, (n, GROUP_W), 1)

    for g in range(SSM_GROUPS):
        b_g = bc_ref[:, g * SSM_STATE:(g + 1) * SSM_STATE]
        c_g = bc_ref[:, SSM_GROUPS * SSM_STATE + g * SSM_STATE:SSM_GROUPS * SSM_STATE + (g + 1) * SSM_STATE]
        x_g = x_ref[:, g * GROUP_W:(g + 1) * GROUP_W]
        x_gb = x_g.astype(BF16)
        s_g = s_ref[g]
        cb = lax.dot_general(c_g, b_g, (((1,), (1,)), ((), ())), preferred_element_type=F32)

        m_parts, colbs, dtcols = [], [], []
        for e in range(HEADS_PER_GROUP):
            h = g * HEADS_PER_GROUP + e
            colb = jnp.broadcast_to(acum[:, h_off + h:h_off + h + 1], (n, n))
            rowb = jnp.broadcast_to(acum_t[h:h + 1, :], (n, n))
            dtrow = jnp.broadcast_to(dtt[h:h + 1, :], (n, n))
            decay = jnp.exp(jnp.where(tri, colb - rowb, -jnp.inf))
            m_parts.append((cb * decay * dtrow).astype(BF16))
            colbs.append(colb)
            dtcols.append(jnp.broadcast_to(dt[:, h_off + h:h_off + h + 1], (n, n)))

        ecols, wcols, etots = [], [], []
        for pair in range(HEADS_PER_GROUP // 2):
            colb = jnp.where(lane_lo, colbs[2 * pair], colbs[2 * pair + 1])
            dtcol = jnp.where(lane_lo, dtcols[2 * pair], dtcols[2 * pair + 1])
            ecol = jnp.exp(colb)
            tot = jnp.broadcast_to(colb[edge:edge + 1, :], (n, n))
            ecols.append(ecol)
            wcols.append(jnp.exp(tot - colb) * dtcol)
            etots.append(ecol[edge:edge + 1, :])
        scale_e = jnp.concatenate(ecols, axis=1)
        scale_w = jnp.concatenate(wcols, axis=1)
        scale_t = jnp.concatenate(etots, axis=1)

        y = None
        for pair in range(HEADS_PER_GROUP // 2):
            e0, e1 = 2 * pair, 2 * pair + 1
            lhs = jnp.concatenate([m_parts[e0], m_parts[e1]], axis=1)
            r0 = jnp.where((lane_w // SSM_HEADDIM) == e0, x_gb, jnp.zeros_like(x_gb))
            r1 = jnp.where((lane_w // SSM_HEADDIM) == e1, x_gb, jnp.zeros_like(x_gb))
            part = jnp.dot(lhs, jnp.concatenate([r0, r1], axis=0), preferred_element_type=F32)
            y = part if y is None else y + part
        y = y + jnp.dot(c_g, s_g.astype(BF16), preferred_element_type=F32) * scale_e
        xw = (x_g * scale_w).astype(BF16)
        b_t = jnp.transpose(b_g.astype(F32)).astype(BF16)
        s_ref[g] = s_g * scale_t + jnp.dot(b_t, xw, preferred_element_type=F32)

        cols = slice(g * GROUP_W, (g + 1) * GROUP_W)
        if final:
            y = y + yb_ref[:, cols] + x_g * dsk_ref[:, cols]
            gz = y * _silu(z_ref[:, cols])
            ms = jnp.mean(gz * gz, axis=-1, keepdims=True)
            o_ref[:, cols] = (gz * lax.rsqrt(ms + LN_EPS) * nw_ref[:, cols]).astype(o_ref.dtype)
        else:
            o_ref[:, cols] = y


def _ssd_pass(geom, x, bc, dtr, dtt, dt_bias, a_log, rev, final_args=None):
    ncl = geom.seq // CHUNK
    ncc = geom.ctx // CHUNK
    lat_chunks = geom.n_lat // CHUNK

    def chunk(b, s):
        if rev:
            return jnp.where(s < ncc, lat_chunks + b * ncc + (ncc - 1 - s), b * ncl + (ncl - 1 - (s - ncc)))
        return jnp.where(s < ncc, lat_chunks + b * ncc + s, b * ncl + (s - ncc))

    def rows(width):
        return pl.BlockSpec((CHUNK, width), lambda b, s: (chunk(b, s), 0))

    d_idx = 1 if rev else 0
    small = lambda shape: pl.BlockSpec(shape, lambda b, s: (0, 0))
    in_specs = [rows(D_SSM), rows(D_SSM), rows(LANES),
                pl.BlockSpec((SSM_HEADS, CHUNK), lambda b, s: (d_idx, chunk(b, s))),
                small((1, LANES)), small((SSM_HEADS, 1)), small((1, LANES)), small((SSM_HEADS, 1))]
    lane_pad = (d_idx * SSM_HEADS, LANES - (d_idx + 1) * SSM_HEADS)
    args = [x, bc, dtr, dtt, jnp.pad(dt_bias, lane_pad).reshape(1, LANES), dt_bias.reshape(SSM_HEADS, 1),
            jnp.pad(a_log, lane_pad).reshape(1, LANES), a_log.reshape(SSM_HEADS, 1)]
    final = final_args is not None
    if final:
        y_other, proj, dsk, nw = final_args
        in_specs += [rows(D_SSM), rows(D_SSM), small((1, D_SSM)), small((1, D_SSM))]
        args += [y_other, proj, dsk, nw]
    return pl.pallas_call(
        functools.partial(_ssd_kernel, rev=rev, final=final),
        grid=(geom.bsz, ncc + ncl),
        in_specs=in_specs,
        out_specs=rows(D_SSM),
        out_shape=jax.ShapeDtypeStruct((geom.n_all, D_SSM), BF16 if final else F32),
        scratch_shapes=[pltpu.VMEM((SSM_GROUPS, SSM_STATE, GROUP_W), F32)],
        compiler_params=_cparams(("arbitrary", "arbitrary")),
        name="ssd_fwd" if final else "ssd_bwd",
    )(*args)


CONF_LANE_BLK = 512
CONF_SLOT = 2048


def _shift_matrix(run_len):
    l = jnp.arange(run_len)[:, None, None]
    k = jnp.arange(CONF_WIDTH)[None, :, None]
    j = jnp.arange(run_len)[None, None, :]
    return (j == l + k - CONF_HALF).astype(BF16).reshape(run_len, CONF_WIDTH * run_len)


def _conf_rows_kernel(val_ref, gate_ref, s_run_ref, s_ctx_ref, w_ref, cb_ref, g_ref, b_ref, o_ref, rhs_ref, res_ref,
                      *, lat_tiles):
    i = pl.program_id(0)

    def run_conv(run_len, n_runs, s_ref, slots):
        span = CONF_WIDTH * run_len
        it = 0
        for c0 in range(0, D_CONF, CONF_LANE_BLK):
            lanes = slice(c0, c0 + CONF_LANE_BLK)
            for r in range(n_runs):
                rows = slice(r * run_len, (r + 1) * run_len)
                base = (it % slots) * CONF_SLOT
                it += 1
                glu = (val_ref[rows, lanes] * _sigmoid(gate_ref[rows, lanes])).astype(BF16)
                for k in range(CONF_WIDTH):
                    rhs_ref[base + k * run_len:base + (k + 1) * run_len, :] = (
                        glu * w_ref[k:k + 1, lanes].astype(BF16))
                res_ref[rows, lanes] = (jnp.dot(s_ref[...], rhs_ref[base:base + span, :],
                                                preferred_element_type=F32) + cb_ref[:, lanes])

    @pl.when(i < lat_tiles)
    def _():
        run_conv(GRID_W, ROW_TILE // GRID_W, s_run_ref, 2)

    @pl.when(i >= lat_tiles)
    def _():
        run_conv(ROW_TILE, 1, s_ctx_ref, 1)

    rows_blk = 64
    for r0 in range(0, ROW_TILE, rows_blk):
        v = _standardize(res_ref[r0:r0 + rows_blk, :]) * g_ref[...] + b_ref[...]
        o_ref[r0:r0 + rows_blk, :] = _silu(v).astype(o_ref.dtype)


def _conformer_rows(geom, proj, w, cb, ln_g, ln_b):
    assert geom.ctx == ROW_TILE and CONF_SLOT >= CONF_WIDTH * GRID_W
    n_tiles = geom.n_all // ROW_TILE
    full = lambda shape: pl.BlockSpec(shape, lambda i: (0,) * len(shape))
    return pl.pallas_call(
        functools.partial(_conf_rows_kernel, lat_tiles=geom.lat_tiles),
        grid=(n_tiles,),
        in_specs=[pl.BlockSpec((ROW_TILE, D_CONF), lambda i: (i, COL_VAL)),
                  pl.BlockSpec((ROW_TILE, D_CONF), lambda i: (i, COL_GATE)),
                  full((GRID_W, CONF_WIDTH * GRID_W)), full((ROW_TILE, CONF_WIDTH * ROW_TILE)),
                  full((CONF_WIDTH, D_CONF)), full((1, D_CONF)), full((1, D_CONF)), full((1, D_CONF))],
        out_specs=pl.BlockSpec((ROW_TILE, D_CONF), lambda i: (i, 0)),
        out_shape=jax.ShapeDtypeStruct((geom.n_all, D_CONF), BF16),
        scratch_shapes=[pltpu.VMEM((CONF_WIDTH * ROW_TILE, CONF_LANE_BLK), BF16),
                        pltpu.VMEM((ROW_TILE, D_CONF), F32)],
        compiler_params=_cparams(("arbitrary",)),
        name="conformer_rows",
    )(proj, proj, _shift_matrix(GRID_W), _shift_matrix(ROW_TILE), w, cb.reshape(1, D_CONF),
      ln_g.reshape(1, D_CONF), ln_b.reshape(1, D_CONF))


def _conf_cols_kernel(val_ref, gate_ref, w_ref, cb_ref, o_ref, scr_ref, *, seq):
    halo = CONF_HALF * GRID_W
    scr_ref[0:halo, :] = jnp.zeros((halo, LANES), F32)
    scr_ref[halo + seq:2 * halo + seq, :] = jnp.zeros((halo, LANES), F32)
    blk = 256
    for r0 in range(0, seq, blk):
        scr_ref[halo + r0:halo + r0 + blk, :] = val_ref[r0:r0 + blk, :] * _sigmoid(gate_ref[r0:r0 + blk, :])
    rows_blk = 128

    def body(rb, carry):
        acc = jnp.broadcast_to(cb_ref[...], (rows_blk, LANES))
        for k in range(CONF_WIDTH):
            start = pl.multiple_of(rb * rows_blk + k * GRID_W, GRID_W)
            acc = acc + scr_ref[pl.ds(start, rows_blk), :] * w_ref[k:k + 1, :]
        o_ref[pl.ds(pl.multiple_of(rb * rows_blk, rows_blk), rows_blk), :] = acc
        return carry

    lax.fori_loop(0, seq // rows_blk, body, 0)


def _conformer_cols_conv(geom, proj, w, cb):
    seq = geom.seq
    nblk = D_CONF // LANES
    val0 = COL_VAL * D_CONF // LANES
    gate0 = COL_GATE * D_CONF // LANES
    return pl.pallas_call(
        functools.partial(_conf_cols_kernel, seq=seq),
        grid=(geom.bsz, nblk),
        in_specs=[pl.BlockSpec((seq, LANES), lambda b, c: (b, val0 + c)),
                  pl.BlockSpec((seq, LANES), lambda b, c: (b, gate0 + c)),
                  pl.BlockSpec((CONF_WIDTH, LANES), lambda b, c: (0, c)),
                  pl.BlockSpec((1, LANES), lambda b, c: (0, c))],
        out_specs=pl.BlockSpec((seq, LANES), lambda b, c: (b, c)),
        out_shape=jax.ShapeDtypeStruct((geom.n_lat, D_CONF), F32),
        scratch_shapes=[pltpu.VMEM((seq + 2 * CONF_HALF * GRID_W, LANES), F32)],
        compiler_params=_cparams(("arbitrary", "arbitrary")),
        name="conformer_cols",
    )(proj, proj, w, cb.reshape(1, D_CONF))


def _ln_swish_kernel(v_ref, g_ref, b_ref, o_ref):
    o_ref[...] = _silu(_standardize(v_ref[...]) * g_ref[...] + b_ref[...]).astype(o_ref.dtype)


def _ln_swish(conv, ln_g, ln_b):
    rows = conv.shape[0]
    full = pl.BlockSpec((1, D_CONF), lambda i: (0, 0))
    return pl.pallas_call(
        _ln_swish_kernel,
        grid=(rows // ROW_TILE,),
        in_specs=[pl.BlockSpec((ROW_TILE, D_CONF), lambda i: (i, 0)), full, full],
        out_specs=pl.BlockSpec((ROW_TILE, D_CONF), lambda i: (i, 0)),
        out_shape=jax.ShapeDtypeStruct((rows, D_CONF), BF16),
        compiler_params=_cparams(("arbitrary",)),
        name="conformer_ln",
    )(conv, ln_g.reshape(1, D_CONF), ln_b.reshape(1, D_CONF))


def _top2(logits):
    lane_i = lax.broadcasted_iota(jnp.int32, logits.shape, 1)
    lane = lane_i.astype(F32)
    m1 = jnp.max(logits, axis=-1, keepdims=True)
    i1 = jnp.min(jnp.where(logits == m1, lane, float(LANES)), axis=-1, keepdims=True)
    rest = jnp.where(lane == i1, -jnp.inf, logits)
    m2 = jnp.max(rest, axis=-1, keepdims=True)
    i2 = jnp.min(jnp.where(rest == m2, lane, float(LANES)), axis=-1, keepdims=True)
    e2 = jnp.exp(m2 - m1)
    w1 = 1.0 / (1.0 + e2)
    w2 = e2 / (1.0 + e2)
    idx = jnp.where(lane_i == 0, i1, jnp.where(lane_i == 1, i2, 0.0)).astype(jnp.int32)
    wts = jnp.where(lane_i == 0, w1, jnp.where(lane_i == 1, w2, 0.0))
    return idx, wts


HALF_D = D_MODEL // 2
_HI_MASK = 0xFFFF0000


def _pack_bf16_pairs(u):
    bits = pltpu.bitcast(u.astype(BF16).astype(F32), jnp.uint32)
    return (bits[:, HALF_D:] & jnp.uint32(_HI_MASK)) | (bits[:, :HALF_D] >> 16)


def _unpack_bf16_pairs(p):
    lo = pltpu.bitcast(p << 16, F32).astype(BF16)
    hi = pltpu.bitcast(p & jnp.uint32(_HI_MASK), F32).astype(BF16)
    return lo, hi


def _residual_norm(h, f, gate, ln_g, ln_b):
    return _standardize(ALPHA * h + gate * f) * ln_g + ln_b


def _epilogue_kernel(*refs, modulate, router, lat_tiles):
    if lat_tiles is not None:
        h_in = jnp.where(pl.program_id(0) < lat_tiles, refs[0][...], refs[1][...])
        refs = refs[1:]
    else:
        h_in = refs[0][...]
    _, f_ref, gate_ref, lg_ref, lb_ref = refs[:5]
    pos = 5
    if modulate:
        sh_ref, sc_ref = refs[pos:pos + 2]
        pos += 2
    if router:
        rwh_ref, rwl_ref, rb_ref = refs[pos:pos + 3]
        pos += 3
    outs = refs[pos:]
    hn = _residual_norm(h_in, f_ref[...], gate_ref[...], lg_ref[...], lb_ref[...])
    outs[0][...] = hn
    if modulate:
        u = _standardize(hn) * (1.0 + sc_ref[...]) + sh_ref[...]
        if router:
            outs[1][...] = _pack_bf16_pairs(u)
        else:
            outs[1][...] = u.astype(BF16)
    if router:
        u_hi = u.astype(BF16)
        u_lo = (u - u_hi.astype(F32)).astype(BF16)
        logits = (jnp.dot(u_hi, rwh_ref[...], preferred_element_type=F32)
                  + jnp.dot(u_lo, rwh_ref[...], preferred_element_type=F32)
                  + jnp.dot(u_hi, rwl_ref[...], preferred_element_type=F32)) + rb_ref[...]
        lane = lax.broadcasted_iota(jnp.int32, logits.shape, 1)
        logits = jnp.where(lane < N_EXPERTS, logits, -jnp.inf)
        idx, wts = _top2(logits)
        outs[2][...] = idx
        outs[3][...] = wts


def _epilogue(geom, h, f, rows, mods, gate_slot, ln_g, ln_b, next_mods=None, next_slots=None, router=None):
    n_tiles = rows // ROW_TILE
    row = pl.BlockSpec((ROW_TILE, D_MODEL), lambda i: (i, 0))
    vec = pl.BlockSpec((1, D_MODEL), lambda i: (0, 0))
    lat_tiles = None
    if isinstance(h, tuple):
        lat_tiles = geom.lat_tiles
        h_specs, h_args = _split_row_specs(lat_tiles), list(h)
    else:
        h_specs, h_args = [row], [h]
    in_specs = h_specs + [row, _mod_spec(geom, gate_slot), vec, vec]
    args = h_args + [f, mods, ln_g.reshape(1, D_MODEL), ln_b.reshape(1, D_MODEL)]
    out_specs = [row]
    out_shape = [jax.ShapeDtypeStruct((rows, D_MODEL), F32)]
    modulate = next_mods is not None
    if modulate:
        in_specs += [_mod_spec(geom, next_slots[0]), _mod_spec(geom, next_slots[1])]
        args += [next_mods, next_mods]
        if router is not None:
            out_specs.append(pl.BlockSpec((ROW_TILE, HALF_D), lambda i: (i, 0)))
            out_shape.append(jax.ShapeDtypeStruct((rows, HALF_D), jnp.uint32))
        else:
            out_specs.append(row)
            out_shape.append(jax.ShapeDtypeStruct((rows, D_MODEL), BF16))
    if router is not None:
        rw_hi, rw_lo, rb = router
        small = pl.BlockSpec((D_MODEL, LANES), lambda i: (0, 0))
        in_specs += [small, small, pl.BlockSpec((1, LANES), lambda i: (0, 0))]
        args += [rw_hi, rw_lo, rb]
        lane_blk = pl.BlockSpec((ROW_TILE, LANES), lambda i: (i, 0))
        out_specs += [lane_blk, lane_blk]
        out_shape += [jax.ShapeDtypeStruct((rows, LANES), jnp.int32),
                      jax.ShapeDtypeStruct((rows, LANES), F32)]
    return pl.pallas_call(
        functools.partial(_epilogue_kernel, modulate=modulate, router=router is not None, lat_tiles=lat_tiles),
        grid=(n_tiles,),
        in_specs=in_specs,
        out_specs=out_specs,
        out_shape=out_shape,
        compiler_params=_cparams(("arbitrary",)),
        name="epilogue",
    )(*args)


GATHER_ROWS = 256


def _row_copy(src_hbm, dst_vmem, sem, src_row, dst_row):
    return pltpu.make_async_copy(src_hbm.at[pl.ds(src_row, 1), :], dst_vmem.at[pl.ds(dst_row, 1), :], sem)


def _issue_rows(src_ref, dst_refs, sem, idx_fn, n_rows):
    def body(r, carry):
        for p, dst in enumerate(dst_refs):
            _row_copy(src_ref, dst, sem, idx_fn(len(dst_refs) * r + p), r).start(priority=p)
        return carry

    lax.fori_loop(0, n_rows, body, 0, unroll=8)


def _drain_rows(src_ref, dst_refs, sem, n_rows):
    for dst in dst_refs:
        pltpu.make_async_copy(src_ref.at[pl.ds(0, n_rows), :], dst, sem).wait()


def _double_buffered_rows(src_ref, bufs_of_slot, sem, cur_idx_ref, nxt_idx_ref, n_rows):
    i = pl.program_id(0)
    slot = i % 2

    @pl.when(i == 0)
    def _():
        _issue_rows(src_ref, bufs_of_slot(0), sem.at[0], lambda q: cur_idx_ref[0, 0, q], n_rows)

    @pl.when(i + 1 < pl.num_programs(0))
    def _():
        _issue_rows(src_ref, bufs_of_slot(1 - slot), sem.at[1 - slot], lambda q: nxt_idx_ref[0, 0, q], n_rows)

    _drain_rows(src_ref, bufs_of_slot(slot), sem.at[slot], n_rows)
    return slot


def _idx_specs(per_step, steps):
    cur = pl.BlockSpec((1, 1, per_step), lambda i: (i, 0, 0), memory_space=pltpu.SMEM)
    nxt = pl.BlockSpec((1, 1, per_step), lambda i: (jnp.minimum(i + 1, steps - 1), 0, 0), memory_space=pltpu.SMEM)
    return cur, nxt


def _gather_kernel(idx_ref, nxt_ref, src_ref, o_ref, buf_ref, sem):
    half = GATHER_ROWS // 2

    def bufs(slot):
        return (buf_ref.at[slot, 0], buf_ref.at[slot, 1])

    slot = _double_buffered_rows(src_ref, bufs, sem, idx_ref, nxt_ref, half)
    for p in range(2):
        lo, hi = _unpack_bf16_pairs(buf_ref[slot, p])
        o_ref[p, :, :HALF_D] = lo
        o_ref[p, :, HALF_D:] = hi


def _gather_rows(src, slot_token):
    n_slots = slot_token.shape[0]
    width = src.shape[1]
    steps = n_slots // GATHER_ROWS
    half = GATHER_ROWS // 2
    idx = jnp.transpose(slot_token.reshape(steps, 2, half), (0, 2, 1)).reshape(steps, 1, GATHER_ROWS)
    cur, nxt = _idx_specs(GATHER_ROWS, steps)
    out = pl.pallas_call(
        _gather_kernel,
        grid=(steps,),
        in_specs=[cur, nxt, pl.BlockSpec(memory_space=pl.ANY)],
        out_specs=pl.BlockSpec((None, 2, half, 2 * width), lambda i: (i, 0, 0, 0)),
        out_shape=jax.ShapeDtypeStruct((steps, 2, half, 2 * width), BF16),
        scratch_shapes=[pltpu.VMEM((2, 2, half, width), src.dtype), pltpu.SemaphoreType.DMA((2,))],
        compiler_params=_cparams(("arbitrary",)),
        name="moe_gather",
    )(idx, idx, src)
    return out.reshape(n_slots, 2 * width)


def _tile_idx(w, nused_ref):
    return jnp.minimum(w, nused_ref[0] - 1)


def _fresh_expert(te_ref, w):
    return (w == 0) | (te_ref[w] != te_ref[jnp.maximum(w - 1, 0)])


def _moe_up_kernel(te_ref, nused_ref, a_ref, w1_ref, w3_ref, o_ref, w1b_ref, w3b_ref):
    w = pl.program_id(1)
    used = w < nused_ref[0]

    @pl.when(used & _fresh_expert(te_ref, w))
    def _():
        w1b_ref[...] = w1_ref[...].astype(BF16)
        w3b_ref[...] = w3_ref[...].astype(BF16)

    @pl.when(used)
    def _():
        a = a_ref[...]
        h1 = jnp.dot(a, w1b_ref[...], preferred_element_type=F32)
        h3 = jnp.dot(a, w3b_ref[...], preferred_element_type=F32)
        o_ref[...] = (_silu(h1) * h3).astype(o_ref.dtype)

    @pl.when(jnp.logical_not(used))
    def _():
        o_ref[...] = jnp.zeros_like(o_ref)


MOE_UP_VMEM = 59 * 1024 * 1024


def _moe_up(a_sorted, w1, w3, tile_expert, n_used, tf=512):
    n_slots, k = a_sorted.shape
    n_tiles = n_slots // MOE_TM
    f = w1.shape[2]
    w_map = lambda j, w, te, nu: (te[_tile_idx(w, nu)], 0, j)
    grid_spec = pltpu.PrefetchScalarGridSpec(
        num_scalar_prefetch=2,
        grid=(f // tf, n_tiles),
        in_specs=[pl.BlockSpec((MOE_TM, k), lambda j, w, te, nu: (_tile_idx(w, nu), 0)),
                  pl.BlockSpec((None, k, tf), w_map),
                  pl.BlockSpec((None, k, tf), w_map)],
        out_specs=pl.BlockSpec((MOE_TM, tf), lambda j, w, te, nu: (w, j)),
        scratch_shapes=[pltpu.VMEM((k, tf), BF16), pltpu.VMEM((k, tf), BF16)],
    )
    return pl.pallas_call(
        _moe_up_kernel,
        grid_spec=grid_spec,
        out_shape=jax.ShapeDtypeStruct((n_slots, f), BF16),
        compiler_params=_cparams(("arbitrary", "arbitrary"), MOE_UP_VMEM),
        name="moe_up",
    )(tile_expert, n_used, a_sorted, w1, w3)


def _moe_down_kernel(te_ref, nused_ref, a_ref, w_ref, o_ref, wb_ref):
    w = pl.program_id(1)
    used = w < nused_ref[0]

    @pl.when(used & _fresh_expert(te_ref, w))
    def _():
        wb_ref[...] = w_ref[...].astype(BF16)

    @pl.when(used)
    def _():
        o_ref[...] = jnp.dot(a_ref[...], wb_ref[...], preferred_element_type=F32)

    @pl.when(jnp.logical_not(used))
    def _():
        o_ref[...] = jnp.zeros_like(o_ref)


def _moe_down(hid, w2, tile_expert, n_used, tn=1024):
    n_slots, k = hid.shape
    n_tiles = n_slots // MOE_TM
    n = w2.shape[2]
    grid_spec = pltpu.PrefetchScalarGridSpec(
        num_scalar_prefetch=2,
        grid=(n // tn, n_tiles),
        in_specs=[pl.BlockSpec((MOE_TM, k), lambda j, w, te, nu: (_tile_idx(w, nu), 0)),
                  pl.BlockSpec((None, k, tn), lambda j, w, te, nu: (te[_tile_idx(w, nu)], 0, j))],
        out_specs=pl.BlockSpec((MOE_TM, tn), lambda j, w, te, nu: (w, j)),
        scratch_shapes=[pltpu.VMEM((k, tn), BF16)],
    )
    return pl.pallas_call(
        _moe_down_kernel,
        grid_spec=grid_spec,
        out_shape=jax.ShapeDtypeStruct((n_slots, n), F32),
        compiler_params=_cparams(("arbitrary", "arbitrary")),
        name="moe_down",
    )(tile_expert, n_used, hid, w2)


COMBINE_ROWS = 128


def _combine_kernel(pos_ref, nxt_ref, h_ref, wts_ref, gate_ref, lg_ref, lb_ref, y_ref, o_ref, buf_ref, sem):
    def bufs(slot):
        return (buf_ref.at[slot, 0], buf_ref.at[slot, 1])

    slot = _double_buffered_rows(y_ref, bufs, sem, pos_ref, nxt_ref, COMBINE_ROWS)
    wts = wts_ref[...]
    f = buf_ref[slot, 0] * wts[:, 0:1] + buf_ref[slot, 1] * wts[:, 1:2]
    o_ref[...] = _residual_norm(h_ref[...], f, gate_ref[...], lg_ref[...], lb_ref[...])


def _moe_combine(geom, h, y_sorted, pos, wts, mods, gate_slot, ln_g, ln_b):
    rows = geom.n_lat
    steps = rows // COMBINE_ROWS
    per_mod = ROW_TILE // COMBINE_ROWS
    row = pl.BlockSpec((COMBINE_ROWS, D_MODEL), lambda i: (i, 0))
    vec = pl.BlockSpec((1, D_MODEL), lambda i: (0, 0))
    cur, nxt = _idx_specs(2 * COMBINE_ROWS, steps)
    pos3 = pos.reshape(steps, 1, 2 * COMBINE_ROWS)
    return pl.pallas_call(
        _combine_kernel,
        grid=(steps,),
        in_specs=[cur, nxt, row,
                  pl.BlockSpec((COMBINE_ROWS, LANES), lambda i: (i, 0)),
                  pl.BlockSpec((None, 1, D_MODEL), lambda i: (geom.mod_row(i // per_mod) * 6 + gate_slot, 0, 0)),
                  vec, vec,
                  pl.BlockSpec(memory_space=pl.ANY)],
        out_specs=row,
        out_shape=jax.ShapeDtypeStruct((rows, D_MODEL), F32),
        scratch_shapes=[pltpu.VMEM((2, 2, COMBINE_ROWS, D_MODEL), F32), pltpu.SemaphoreType.DMA((2,))],
        compiler_params=_cparams(("arbitrary",)),
        name="moe_combine",
    )(pos3, pos3, h, wts, mods, ln_g.reshape(1, D_MODEL), ln_b.reshape(1, D_MODEL), y_sorted)


def _route(top_idx, n_tokens):
    flat = top_idx.reshape(-1)
    onehot = (jnp.arange(N_EXPERTS, dtype=jnp.int32)[:, None] == flat[None, :]).astype(jnp.int32)
    blk = 2 * ROW_TILE
    inner = jnp.cumsum(onehot.reshape(N_EXPERTS, -1, blk), axis=2)
    blk_tot = inner[:, :, -1]
    blk_off = jnp.cumsum(blk_tot, axis=1) - blk_tot
    csum = (inner + blk_off[:, :, None]).reshape(N_EXPERTS, -1)
    rank = jnp.sum(csum * onehot, axis=0) - 1
    counts = csum[:, -1]
    tiles_per = (counts + MOE_TM - 1) // MOE_TM
    tile_end = jnp.cumsum(tiles_per)
    tile_start = tile_end - tiles_per
    pos = (jnp.sum(onehot * (tile_start * MOE_TM)[:, None], axis=0) + rank).astype(jnp.int32)
    n_tiles = (2 * n_tokens) // MOE_TM + N_EXPERTS
    n_used = tile_end[-1].astype(jnp.int32)
    tile_ids = jnp.arange(n_tiles, dtype=jnp.int32)
    tile_expert = jnp.sum((tile_ids[:, None] >= tile_end[None, :]).astype(jnp.int32), axis=1)
    tile_expert = jnp.minimum(tile_expert, N_EXPERTS - 1).astype(jnp.int32)
    slot_token = jnp.zeros((n_tiles * MOE_TM,), jnp.int32).at[pos].set(
        jnp.arange(2 * n_tokens, dtype=jnp.int32) // 2)
    return pos, slot_token, tile_expert, n_used.reshape(1)


def kernel(x, c, ctx, c_ctx, ada_w, ada_b, w_in, mamba_conv_w, mamba_conv_b, dt_bias_fwd, dt_bias_bwd, a_log_fwd, a_log_bwd, d_skip, ssm_norm_w, conf_conv_w, conf_conv_b, conf_ln_g, conf_ln_b, w_out, ln1_g, ln1_b, ln2_g, ln2_b, ffn_w1, ffn_w3, ffn_w2, router_w, router_b, moe_w1, moe_w3, moe_w2):
    bsz, seq, d = x.shape
    ctx_len = ctx.shape[1]
    geom = _Geom(bsz, seq, ctx_len)
    n_all, n_lat = geom.n_all, geom.n_lat

    mod_rows = SUBLANES
    cond_rows = jnp.zeros((mod_rows, d), F32).at[:bsz].set(c).at[bsz].set(c_ctx)
    mods_all = _ada_mods(cond_rows, ada_w, ada_b).reshape(DEPTH, mod_rows * 6, 1, d)

    h = (x.reshape(n_lat, d), ctx.reshape(bsz * ctx_len, d))
    u = _pack_modulate(geom, h[0], h[1], mods_all[0])

    tm_all = _pick_tile(n_all, 1056)
    tm_lat = _pick_tile(n_lat, 1024)

    w_out_all = w_out.astype(BF16)

    for i in range(DEPTH):
        mods = mods_all[i]
        last = i == DEPTH - 1
        w_main = jnp.concatenate([w_in[i][:, :OFF_DT], w_in[i][:, OFF_X:]], axis=1).astype(BF16)
        w_dt = jnp.pad(w_in[i][:, OFF_DT:OFF_X], ((0, 0), (0, LANES - 2 * SSM_HEADS))).astype(BF16)
        proj = _matmul([(u, w_main, 0)], n_all, tm_all, 1024, F32, "in_proj")
        dtr = _matmul([(u, w_dt, 0)], n_all, tm_all, LANES, F32, "dt_proj")
        dtt = jnp.transpose(dtr[:, :2 * SSM_HEADS])

        xs, bc = _ssm_conv(geom, proj, mamba_conv_w[i], mamba_conv_b[i])
        y_b = _ssd_pass(geom, xs, bc, dtr, dtt, dt_bias_bwd[i], a_log_bwd[i], rev=True)
        dsk = jnp.repeat(d_skip[i], SSM_HEADDIM).reshape(1, D_SSM)
        y_ssm = _ssd_pass(geom, xs, bc, dtr, dtt, dt_bias_fwd[i], a_log_fwd[i], rev=False,
                          final_args=(y_b, proj, dsk, ssm_norm_w[i].reshape(1, D_SSM)))

        if i % 2 == 0:
            v = _conformer_rows(geom, proj, conf_conv_w[i], conf_conv_b[i], conf_ln_g[i], conf_ln_b[i])
        else:
            conv = _conformer_cols_conv(geom, proj, conf_conv_w[i], conf_conv_b[i])
            v = _ln_swish(conv, conf_ln_g[i], conf_ln_b[i])

        rows = n_lat if last else n_all
        tm = tm_lat if last else tm_all
        mix = _matmul([(y_ssm, w_out_all, (i, 0)), (v, w_out_all, (i, 1))], rows, tm, 1024, F32, "out_proj")

        j = i // 2
        if i % 2 == 0:
            h, u2 = _epilogue(geom, h, mix, rows, mods, 2, ln1_g[i], ln1_b[i], mods, (3, 4))
            w2 = ffn_w2[j].astype(BF16)
            hid = _gated_up(u2, ffn_w1[j], ffn_w3[j], rows, _pick_tile(rows, 1408), 256, FFN_PAD)
            f = _matmul_ktiled(hid, w2, rows, tm, 2048, 1024, "ffn_down")
            if last:
                h = _epilogue(geom, h, f, rows, mods, 5, ln2_g[i], ln2_b[i])[0]
            else:
                h, u = _epilogue(geom, h, f, rows, mods, 5, ln2_g[i], ln2_b[i], mods_all[i + 1], (0, 1))
        else:
            rw = jnp.pad(router_w[j], ((0, 0), (0, LANES - N_EXPERTS)))
            rw_hi = rw.astype(BF16)
            rw_lo = (rw - rw_hi.astype(F32)).astype(BF16)
            rb = jnp.pad(router_b[j], (0, LANES - N_EXPERTS)).reshape(1, LANES)
            assert last, "a routed layer is only supported as the final layer"
            h, u2p, top_idx, top_w = _epilogue(geom, h, mix, rows, mods, 2, ln1_g[i], ln1_b[i], mods, (3, 4),
                                                  router=(rw_hi, rw_lo, rb))
            pos, slot_token, tile_expert, n_used = _route(top_idx[:, :2], rows)
            a_sorted = _gather_rows(u2p, slot_token)
            hid = _moe_up(a_sorted, moe_w1[j], moe_w3[j], tile_expert, n_used)
            y_sorted = _moe_down(hid, moe_w2[j], tile_expert, n_used)
            h = _moe_combine(geom, h, y_sorted, pos, top_w, mods, 5, ln2_g[i], ln2_b[i])

    return h[:n_lat].reshape(bsz, seq, d)
```

```python
import functools

import jax
import jax.numpy as jnp
from jax import lax
from jax.experimental import pallas as pl
from jax.experimental.pallas import tpu as pltpu

F32 = jnp.float32
BF16 = jnp.bfloat16

D_MODEL = 4096
DEPTH = 2
GRID_W = 64
D_SSM = 2048
D_CONF = 2048
SSM_HEADDIM = 64
SSM_HEADS = 32
SSM_GROUPS = 8
HEADS_PER_GROUP = 4
SSM_STATE = 128
SSM_CONV = 5
CHUNK = 128
GROUP_W = HEADS_PER_GROUP * SSM_HEADDIM
CONF_WIDTH = 31
CONF_HALF = CONF_WIDTH // 2
OFF_DT = D_SSM
OFF_X = OFF_DT + 2 * SSM_HEADS
N_MAIN = 10240
COL_X, COL_BC, COL_VAL, COL_GATE = 1, 2, 3, 4
FFN_DENSE = 11008
FFN_PAD = 11264
N_EXPERTS = 8
FFN_EXPERT = 3584
LN_EPS = 1e-5
ALPHA = (2 * DEPTH) ** 0.25

LANES = 128
SUBLANES = 8
ROW_TILE = 256
MOE_TM = 512
VMEM_LIMIT = 56 * 1024 * 1024


def _cparams(sem, vmem=VMEM_LIMIT):
    return pltpu.CompilerParams(dimension_semantics=sem, vmem_limit_bytes=vmem)


def _sigmoid(x):
    return 1.0 / (1.0 + jnp.exp(-x))


def _silu(x):
    return x * _sigmoid(x)


def _softplus(x):
    return jnp.maximum(x, 0.0) + jnp.log(1.0 + jnp.exp(-jnp.abs(x)))


def _standardize(x):
    mu = jnp.mean(x, axis=-1, keepdims=True)
    xc = x - mu
    var = jnp.mean(xc * xc, axis=-1, keepdims=True)
    return xc * lax.rsqrt(var + LN_EPS)


def _pick_tile(n, target, mult=16):
    best = None
    for t in range(mult, min(n, target) + 1, mult):
        if n % t == 0:
            best = t
    assert best is not None, (n, target)
    return best


def _ada_kernel(c_ref, w_ref, b_ref, o_ref):
    cond = _silu(c_ref[...]).astype(BF16)
    o_ref[...] = jnp.dot(cond, w_ref[...].astype(BF16), preferred_element_type=F32) + b_ref[...]


def _ada_mods(cond_rows, ada_w, ada_b):
    depth, d, n = ada_w.shape
    rows = cond_rows.shape[0]
    tn = 1024
    return pl.pallas_call(
        _ada_kernel,
        grid=(depth, n // tn),
        in_specs=[
            pl.BlockSpec((rows, d), lambda l, j: (0, 0)),
            pl.BlockSpec((None, d, tn), lambda l, j: (l, 0, j)),
            pl.BlockSpec((None, 1, tn), lambda l, j: (l, 0, j)),
        ],
        out_specs=pl.BlockSpec((None, rows, tn), lambda l, j: (l, 0, j)),
        out_shape=jax.ShapeDtypeStruct((depth, rows, n), F32),
        compiler_params=_cparams(("arbitrary", "arbitrary")),
        name="ada_mods",
    )(cond_rows, ada_w, ada_b.reshape(depth, 1, n))


class _Geom:
    def __init__(self, bsz, seq, ctx_len):
        self.bsz, self.seq, self.ctx = bsz, seq, ctx_len
        self.n_lat = bsz * seq
        self.n_all = self.n_lat + bsz * ctx_len
        assert seq % ROW_TILE == 0 and ctx_len % ROW_TILE == 0 and seq % GRID_W == 0
        self.lat_tiles = self.n_lat // ROW_TILE
        self.tiles_per_seq = seq // ROW_TILE
        self.tiles_per_ctx = ctx_len // ROW_TILE

    def mod_row(self, i):
        return jnp.where(i < self.lat_tiles, i // self.tiles_per_seq, self.bsz)

    def seq_edges(self, i):
        lat = i < self.lat_tiles
        j = i - self.lat_tiles
        first = jnp.where(lat, i % self.tiles_per_seq == 0, j % self.tiles_per_ctx == 0)
        last = jnp.where(lat, i % self.tiles_per_seq == self.tiles_per_seq - 1,
                         j % self.tiles_per_ctx == self.tiles_per_ctx - 1)
        return first, last


def _mod_spec(geom, slot):
    return pl.BlockSpec((None, 1, D_MODEL), lambda i: (geom.mod_row(i) * 6 + slot, 0, 0))


def _split_row_specs(lat_tiles):
    return [pl.BlockSpec((ROW_TILE, D_MODEL), lambda i: (jnp.minimum(i, lat_tiles - 1), 0)),
            pl.BlockSpec((ROW_TILE, D_MODEL), lambda i: (jnp.maximum(i - lat_tiles, 0), 0))]


def _pack_mod_kernel(x_ref, c_ref, sh_ref, sc_ref, u_ref, *, lat_tiles):
    v = jnp.where(pl.program_id(0) < lat_tiles, x_ref[...], c_ref[...])
    u_ref[...] = (_standardize(v) * (1.0 + sc_ref[...]) + sh_ref[...]).astype(BF16)


def _pack_modulate(geom, x2, ctx2, mods):
    n_tiles = geom.n_all // ROW_TILE
    return pl.pallas_call(
        functools.partial(_pack_mod_kernel, lat_tiles=geom.lat_tiles),
        grid=(n_tiles,),
        in_specs=_split_row_specs(geom.lat_tiles) + [_mod_spec(geom, 0), _mod_spec(geom, 1)],
        out_specs=pl.BlockSpec((ROW_TILE, D_MODEL), lambda i: (i, 0)),
        out_shape=jax.ShapeDtypeStruct((geom.n_all, D_MODEL), BF16),
        compiler_params=_cparams(("arbitrary",)),
        name="pack_modulate",
    )(x2, ctx2, mods, mods)


W_PREP_ROWS = 256


def _w_in_prep_kernel(w_ref, main_ref, dt_ref):
    main_ref[:, :OFF_DT] = w_ref[:, :OFF_DT].astype(BF16)
    main_ref[:, OFF_DT:] = w_ref[:, OFF_X:].astype(BF16)
    dt_ref[...] = jnp.zeros_like(dt_ref)
    dt_ref[:, :OFF_X - OFF_DT] = w_ref[:, OFF_DT:OFF_X].astype(BF16)


def _w_in_prep(w_in):
    depth, k, n = w_in.shape
    return pl.pallas_call(
        _w_in_prep_kernel,
        grid=(depth, k // W_PREP_ROWS),
        in_specs=[pl.BlockSpec((None, W_PREP_ROWS, n), lambda l, r: (l, r, 0))],
        out_specs=[pl.BlockSpec((None, W_PREP_ROWS, N_MAIN), lambda l, r: (l, r, 0)),
                   pl.BlockSpec((None, W_PREP_ROWS, LANES), lambda l, r: (l, r, 0))],
        out_shape=[jax.ShapeDtypeStruct((depth, k, N_MAIN), BF16),
                   jax.ShapeDtypeStruct((depth, k, LANES), BF16)],
        compiler_params=_cparams(("arbitrary", "arbitrary")),
        name="w_in_prep",
    )(w_in)


def _mm_kernel(*refs, n_pairs):
    o_ref = refs[-1]
    acc = jnp.dot(refs[0][...], refs[1][...], preferred_element_type=F32)
    for p in range(1, n_pairs):
        acc = acc + jnp.dot(refs[2 * p][...], refs[2 * p + 1][...], preferred_element_type=F32)
    o_ref[...] = acc.astype(o_ref.dtype)


def _matmul(pairs, rows, tm, tn, out_dtype, name):
    n = pairs[0][1].shape[-1]
    assert rows % tm == 0 and n % tn == 0
    in_specs, args = [], []
    for a, w, row_blk in pairs:
        k = a.shape[1]
        if isinstance(row_blk, tuple):
            w_spec = pl.BlockSpec((None, k, tn), lambda i, j, rb=row_blk: (rb[0], rb[1], j))
        else:
            w_spec = pl.BlockSpec((k, tn), lambda i, j, rb=row_blk: (rb, j))
        in_specs += [pl.BlockSpec((tm, k), lambda i, j: (i, 0)), w_spec]
        args += [a, w]
    return pl.pallas_call(
        functools.partial(_mm_kernel, n_pairs=len(pairs)),
        grid=(rows // tm, n // tn),
        in_specs=in_specs,
        out_specs=pl.BlockSpec((tm, tn), lambda i, j: (i, j)),
        out_shape=jax.ShapeDtypeStruct((rows, n), out_dtype),
        compiler_params=_cparams(("arbitrary", "arbitrary")),
        name=name,
    )(*args)


def _gated_up_kernel(a_ref, w1_ref, w3_ref, o_ref, *, valid_tiles):
    a = a_ref[...]
    h1 = jnp.dot(a, w1_ref[...].astype(BF16), preferred_element_type=F32)
    h3 = jnp.dot(a, w3_ref[...].astype(BF16), preferred_element_type=F32)
    keep = pl.program_id(1) < valid_tiles
    o_ref[...] = jnp.where(keep, _silu(h1) * h3, 0.0).astype(o_ref.dtype)


def _gated_up(a, w1, w3, rows, tm, tn, n_out):
    k, n_valid = w1.shape
    n = n_out
    assert n_valid % tn == 0 and n % tn == 0
    valid_tiles = n_valid // tn
    w_spec = pl.BlockSpec((k, tn), lambda i, j: (0, jnp.minimum(j, valid_tiles - 1)))
    return pl.pallas_call(
        functools.partial(_gated_up_kernel, valid_tiles=valid_tiles),
        grid=(rows // tm, n // tn),
        in_specs=[pl.BlockSpec((tm, k), lambda i, j: (i, 0)), w_spec, w_spec],
        out_specs=pl.BlockSpec((tm, tn), lambda i, j: (i, j)),
        out_shape=jax.ShapeDtypeStruct((rows, n), BF16),
        compiler_params=_cparams(("arbitrary", "arbitrary")),
        name="ffn_up",
    )(a, w1, w3)


def _mm_acc_kernel(a_ref, w_ref, o_ref, acc_ref, *, k_valid):
    k = pl.program_id(2)
    last = pl.num_programs(2) - 1
    tk = w_ref.shape[0]

    @pl.when(k == 0)
    def _():
        acc_ref[...] = jnp.zeros_like(acc_ref)

    def accumulate(w):
        acc_ref[...] += jnp.dot(a_ref[...], w, preferred_element_type=F32)

    tail = k_valid % tk
    if tail == 0:
        accumulate(w_ref[...])
    else:
        @pl.when(k < last)
        def _():
            accumulate(w_ref[...])

        @pl.when(k == last)
        def _():
            row = lax.broadcasted_iota(jnp.int32, w_ref.shape, 0)
            accumulate(jnp.where(row < tail, w_ref[...], jnp.zeros_like(w_ref)))

    @pl.when(k == last)
    def _():
        o_ref[...] = acc_ref[...]


def _matmul_ktiled(a, w, rows, tm, tn, tk, name):
    kk = a.shape[1]
    k_valid, n = w.shape
    assert rows % tm == 0 and n % tn == 0 and kk % tk == 0 and kk - k_valid < tk
    return pl.pallas_call(
        functools.partial(_mm_acc_kernel, k_valid=k_valid),
        grid=(rows // tm, n // tn, kk // tk),
        in_specs=[pl.BlockSpec((tm, tk), lambda i, j, k: (i, k)),
                  pl.BlockSpec((tk, tn), lambda i, j, k: (k, j))],
        out_specs=pl.BlockSpec((tm, tn), lambda i, j, k: (i, j)),
        out_shape=jax.ShapeDtypeStruct((rows, n), F32),
        scratch_shapes=[pltpu.VMEM((tm, tn), F32)],
        compiler_params=_cparams(("arbitrary", "arbitrary", "arbitrary")),
        name=name,
    )(a, w)


def _conv5_kernel(xm_ref, xp_ref, xn_ref, bm_ref, bp_ref, bn_ref, w_ref, b_ref, xo_ref, bco_ref, scr_ref,
                  *, geom):
    i = pl.program_id(0)
    first, last = geom.seq_edges(i)
    half = D_SSM
    pad = SUBLANES
    for part, (m_ref, p_ref, n_ref) in enumerate(((xm_ref, xp_ref, xn_ref), (bm_ref, bp_ref, bn_ref))):
        lo = part * half
        scr_ref[0:pad, lo:lo + half] = jnp.where(first, 0.0, p_ref[...])
        scr_ref[pad:pad + ROW_TILE, lo:lo + half] = m_ref[...]
        scr_ref[pad + ROW_TILE:2 * pad + ROW_TILE, lo:lo + half] = jnp.where(last, 0.0, n_ref[...])
    rows_blk, lane_blk = 64, 512
    base = pad - SSM_CONV // 2
    for c0 in range(0, 2 * half, lane_blk):
        wc = w_ref[:, c0:c0 + lane_blk]
        bias = b_ref[:, c0:c0 + lane_blk]
        for r0 in range(0, ROW_TILE, rows_blk):
            acc = bias
            for k in range(SSM_CONV):
                acc = acc + scr_ref[base + r0 + k:base + r0 + k + rows_blk, c0:c0 + lane_blk] * wc[k:k + 1, :]
            val = _silu(acc)
            if c0 < half:
                xo_ref[r0:r0 + rows_blk, c0:c0 + lane_blk] = val
            else:
                bco_ref[r0:r0 + rows_blk, c0 - half:c0 - half + lane_blk] = val.astype(BF16)


def _ssm_conv(geom, proj, conv_w, conv_b):
    n_tiles = geom.n_all // ROW_TILE
    sub = ROW_TILE // SUBLANES
    last_blk = geom.n_all // SUBLANES - 1

    def main(col):
        return pl.BlockSpec((ROW_TILE, D_SSM), lambda i: (i, col))

    def prev(col):
        return pl.BlockSpec((SUBLANES, D_SSM), lambda i: (jnp.maximum(i * sub - 1, 0), col))

    def nxt(col):
        return pl.BlockSpec((SUBLANES, D_SSM), lambda i: (jnp.minimum((i + 1) * sub, last_blk), col))

    width = 2 * D_SSM
    return pl.pallas_call(
        functools.partial(_conv5_kernel, geom=geom),
        grid=(n_tiles,),
        in_specs=[main(COL_X), prev(COL_X), nxt(COL_X), main(COL_BC), prev(COL_BC), nxt(COL_BC),
                  pl.BlockSpec((SSM_CONV, width), lambda i: (0, 0)),
                  pl.BlockSpec((1, width), lambda i: (0, 0))],
        out_specs=[pl.BlockSpec((ROW_TILE, D_SSM), lambda i: (i, 0)),
                   pl.BlockSpec((ROW_TILE, D_SSM), lambda i: (i, 0))],
        out_shape=[jax.ShapeDtypeStruct((geom.n_all, D_SSM), F32),
                   jax.ShapeDtypeStruct((geom.n_all, D_SSM), BF16)],
        scratch_shapes=[pltpu.VMEM((ROW_TILE + 2 * SUBLANES, width), F32)],
        compiler_params=_cparams(("arbitrary",)),
        name="ssm_conv",
    )(proj, proj, proj, proj, proj, proj, conv_w, conv_b.reshape(1, width))


def _split3_dot(lhs_fn, v):
    v1 = v.astype(BF16)
    r1 = v - v1.astype(F32)
    v2 = r1.astype(BF16)
    v3 = (r1 - v2.astype(F32)).astype(BF16)
    return lhs_fn(v1) + lhs_fn(v2) + lhs_fn(v3)


def _ssd_kernel(*refs, rev, final):
    if final:
        (x_ref, bc_ref, dtr_ref, dtt_ref, pbr_ref, pbc_ref, alr_ref, alc_ref,
         yb_ref, z_ref, dsk_ref, nw_ref, o_ref, s_ref) = refs
    else:
        x_ref, bc_ref, dtr_ref, dtt_ref, pbr_ref, pbc_ref, alr_ref, alc_ref, o_ref, s_ref = refs
    step = pl.program_id(1)

    @pl.when(step == 0)
    def _():
        s_ref[...] = jnp.zeros_like(s_ref)

    n = CHUNK
    row = lax.broadcasted_iota(jnp.int32, (n, n), 0)
    col = lax.broadcasted_iota(jnp.int32, (n, n), 1)
    tri = (row <= col) if rev else (row >= col)
    cum_l = tri.astype(BF16)
    cum_r = jnp.logical_not(tri) | (row == col)
    cum_r = cum_r.astype(BF16)
    h_off = SSM_HEADS if rev else 0
    edge = 0 if rev else n - 1

    dt = _softplus(dtr_ref[...] + pbr_ref[...])
    a = -dt * jnp.exp(alr_ref[...])
    dtt = _softplus(dtt_ref[...] + pbc_ref[...])
    at = -dtt * jnp.exp(alc_ref[...])
    acum = _split3_dot(lambda v: jnp.dot(cum_l, v, preferred_element_type=F32), a)
    acum_t = _split3_dot(lambda v: jnp.dot(v, cum_r, preferred_element_type=F32), at)

    lane_lo = lax.broadcasted_iota(jnp.int32, (n, LANES), 1) < SSM_HEADDIM
    lane_w = lax.broadcasted_iota(jnp.int32, (n, GROUP_W), 1)

    for g in range(SSM_GROUPS):
        b_g = bc_ref[:, g * SSM_STATE:(g + 1) * SSM_STATE]
        c_g = bc_ref[:, SSM_GROUPS * SSM_STATE + g * SSM_STATE:SSM_GROUPS * SSM_STATE + (g + 1) * SSM_STATE]
        x_g = x_ref[:, g * GROUP_W:(g + 1) * GROUP_W]
        x_gb = x_g.astype(BF16)
        s_g = s_ref[g]
        cb = lax.dot_general(c_g, b_g, (((1,), (1,)), ((), ())), preferred_element_type=F32)

        m_parts, colbs, dtcols = [], [], []
        for e in range(HEADS_PER_GROUP):
            h = g * HEADS_PER_GROUP + e
            colb = jnp.broadcast_to(acum[:, h_off + h:h_off + h + 1], (n, n))
            rowb = jnp.broadcast_to(acum_t[h:h + 1, :], (n, n))
            dtrow = jnp.broadcast_to(dtt[h:h + 1, :], (n, n))
            decay = jnp.exp(jnp.where(tri, colb - rowb, -jnp.inf))
            m_parts.append((cb * decay * dtrow).astype(BF16))
            colbs.append(colb)
            dtcols.append(jnp.broadcast_to(dt[:, h_off + h:h_off + h + 1], (n, n)))

        ecols, wcols, etots = [], [], []
        for pair in range(HEADS_PER_GROUP // 2):
            colb = jnp.where(lane_lo, colbs[2 * pair], colbs[2 * pair + 1])
            dtcol = jnp.where(lane_lo, dtcols[2 * pair], dtcols[2 * pair + 1])
            ecol = jnp.exp(colb)
            tot = jnp.broadcast_to(colb[edge:edge + 1, :], (n, n))
            ecols.append(ecol)
            wcols.append(jnp.exp(tot - colb) * dtcol)
            etots.append(ecol[edge:edge + 1, :])
        scale_e = jnp.concatenate(ecols, axis=1)
        scale_w = jnp.concatenate(wcols, axis=1)
        scale_t = jnp.concatenate(etots, axis=1)

        y = None
        for pair in range(HEADS_PER_GROUP // 2):
            e0, e1 = 2 * pair, 2 * pair + 1
            lhs = jnp.concatenate([m_parts[e0], m_parts[e1]], axis=1)
            r0 = jnp.where((lane_w // SSM_HEADDIM) == e0, x_gb, jnp.zeros_like(x_gb))
            r1 = jnp.where((lane_w // SSM_HEADDIM) == e1, x_gb, jnp.zeros_like(x_gb))
            part = jnp.dot(lhs, jnp.concatenate([r0, r1], axis=0), preferred_element_type=F32)
            y = part if y is None else y + part
        y = y + jnp.dot(c_g, s_g.astype(BF16), preferred_element_type=F32) * scale_e
        xw = (x_g * scale_w).astype(BF16)
        s_ref[g] = s_g * scale_t + lax.dot_general(b_g, xw, (((0,), (0,)), ((), ())),
                                                   preferred_element_type=F32)

        cols = slice(g * GROUP_W, (g + 1) * GROUP_W)
        if final:
            y = y + yb_ref[:, cols] + x_g * dsk_ref[:, cols]
            gz = y * _silu(z_ref[:, cols])
            ms = jnp.mean(gz * gz, axis=-1, keepdims=True)
            o_ref[:, cols] = (gz * lax.rsqrt(ms + LN_EPS) * nw_ref[:, cols]).astype(o_ref.dtype)
        else:
            o_ref[:, cols] = y


def _ssd_pass(geom, x, bc, dtr, dtt, dt_bias, a_log, rev, final_args=None):
    ncl = geom.seq // CHUNK
    ncc = geom.ctx // CHUNK
    lat_chunks = geom.n_lat // CHUNK

    def chunk(b, s):
        if rev:
            return jnp.where(s < ncc, lat_chunks + b * ncc + (ncc - 1 - s), b * ncl + (ncl - 1 - (s - ncc)))
        return jnp.where(s < ncc, lat_chunks + b * ncc + s, b * ncl + (s - ncc))

    def rows(width):
        return pl.BlockSpec((CHUNK, width), lambda b, s: (chunk(b, s), 0))

    d_idx = 1 if rev else 0
    small = lambda shape: pl.BlockSpec(shape, lambda b, s: (0, 0))
    in_specs = [rows(D_SSM), rows(D_SSM), rows(LANES),
                pl.BlockSpec((SSM_HEADS, CHUNK), lambda b, s: (d_idx, chunk(b, s))),
                small((1, LANES)), small((SSM_HEADS, 1)), small((1, LANES)), small((SSM_HEADS, 1))]
    lane_pad = (d_idx * SSM_HEADS, LANES - (d_idx + 1) * SSM_HEADS)
    args = [x, bc, dtr, dtt, jnp.pad(dt_bias, lane_pad).reshape(1, LANES), dt_bias.reshape(SSM_HEADS, 1),
            jnp.pad(a_log, lane_pad).reshape(1, LANES), a_log.reshape(SSM_HEADS, 1)]
    final = final_args is not None
    if final:
        y_other, proj, dsk, nw = final_args
        in_specs += [rows(D_SSM), rows(D_SSM), small((1, D_SSM)), small((1, D_SSM))]
        args += [y_other, proj, dsk, nw]
    return pl.pallas_call(
        functools.partial(_ssd_kernel, rev=rev, final=final),
        grid=(geom.bsz, ncc + ncl),
        in_specs=in_specs,
        out_specs=rows(D_SSM),
        out_shape=jax.ShapeDtypeStruct((geom.n_all, D_SSM), BF16 if final else F32),
        scratch_shapes=[pltpu.VMEM((SSM_GROUPS, SSM_STATE, GROUP_W), F32)],
        compiler_params=_cparams(("arbitrary", "arbitrary")),
        name="ssd_fwd" if final else "ssd_bwd",
    )(*args)


CONF_LANE_BLK = 512
CONF_SLOT = 2048


def _shift_matrix(run_len):
    l = jnp.arange(run_len)[:, None, None]
    k = jnp.arange(CONF_WIDTH)[None, :, None]
    j = jnp.arange(run_len)[None, None, :]
    return (j == l + k - CONF_HALF).astype(BF16).reshape(run_len, CONF_WIDTH * run_len)


def _conf_rows_kernel(val_ref, gate_ref, s_run_ref, s_ctx_ref, w_ref, cb_ref, g_ref, b_ref, o_ref, rhs_ref, res_ref,
                      *, lat_tiles):
    i = pl.program_id(0)

    def run_conv(run_len, n_runs, s_ref, slots):
        span = CONF_WIDTH * run_len
        it = 0
        for c0 in range(0, D_CONF, CONF_LANE_BLK):
            lanes = slice(c0, c0 + CONF_LANE_BLK)
            for r in range(n_runs):
                rows = slice(r * run_len, (r + 1) * run_len)
                base = (it % slots) * CONF_SLOT
                it += 1
                glu = (val_ref[rows, lanes] * _sigmoid(gate_ref[rows, lanes])).astype(BF16)
                for k in range(CONF_WIDTH):
                    rhs_ref[base + k * run_len:base + (k + 1) * run_len, :] = (
                        glu * w_ref[k:k + 1, lanes].astype(BF16))
                res_ref[rows, lanes] = (jnp.dot(s_ref[...], rhs_ref[base:base + span, :],
                                                preferred_element_type=F32) + cb_ref[:, lanes])

    @pl.when(i < lat_tiles)
    def _():
        run_conv(GRID_W, ROW_TILE // GRID_W, s_run_ref, 2)

    @pl.when(i >= lat_tiles)
    def _():
        run_conv(ROW_TILE, 1, s_ctx_ref, 1)

    rows_blk = 64
    for r0 in range(0, ROW_TILE, rows_blk):
        v = _standardize(res_ref[r0:r0 + rows_blk, :]) * g_ref[...] + b_ref[...]
        o_ref[r0:r0 + rows_blk, :] = _silu(v).astype(o_ref.dtype)


def _conformer_rows(geom, proj, w, cb, ln_g, ln_b):
    assert geom.ctx == ROW_TILE and CONF_SLOT >= CONF_WIDTH * GRID_W
    n_tiles = geom.n_all // ROW_TILE
    full = lambda shape: pl.BlockSpec(shape, lambda i: (0,) * len(shape))
    return pl.pallas_call(
        functools.partial(_conf_rows_kernel, lat_tiles=geom.lat_tiles),
        grid=(n_tiles,),
        in_specs=[pl.BlockSpec((ROW_TILE, D_CONF), lambda i: (i, COL_VAL)),
                  pl.BlockSpec((ROW_TILE, D_CONF), lambda i: (i, COL_GATE)),
                  full((GRID_W, CONF_WIDTH * GRID_W)), full((ROW_TILE, CONF_WIDTH * ROW_TILE)),
                  full((CONF_WIDTH, D_CONF)), full((1, D_CONF)), full((1, D_CONF)), full((1, D_CONF))],
        out_specs=pl.BlockSpec((ROW_TILE, D_CONF), lambda i: (i, 0)),
        out_shape=jax.ShapeDtypeStruct((geom.n_all, D_CONF), BF16),
        scratch_shapes=[pltpu.VMEM((CONF_WIDTH * ROW_TILE, CONF_LANE_BLK), BF16),
                        pltpu.VMEM((ROW_TILE, D_CONF), F32)],
        compiler_params=_cparams(("arbitrary",)),
        name="conformer_rows",
    )(proj, proj, _shift_matrix(GRID_W), _shift_matrix(ROW_TILE), w, cb.reshape(1, D_CONF),
      ln_g.reshape(1, D_CONF), ln_b.reshape(1, D_CONF))


def _conf_cols_kernel(val_ref, gate_ref, w_ref, cb_ref, o_ref, scr_ref, *, seq):
    halo = CONF_HALF * GRID_W
    scr_ref[0:halo, :] = jnp.zeros((halo, LANES), F32)
    scr_ref[halo + seq:2 * halo + seq, :] = jnp.zeros((halo, LANES), F32)
    blk = 256
    for r0 in range(0, seq, blk):
        scr_ref[halo + r0:halo + r0 + blk, :] = val_ref[r0:r0 + blk, :] * _sigmoid(gate_ref[r0:r0 + blk, :])
    rows_blk = 128

    def body(rb, carry):
        acc = jnp.broadcast_to(cb_ref[...], (rows_blk, LANES))
        for k in range(CONF_WIDTH):
            start = pl.multiple_of(rb * rows_blk + k * GRID_W, GRID_W)
            acc = acc + scr_ref[pl.ds(start, rows_blk), :] * w_ref[k:k + 1, :]
        o_ref[pl.ds(pl.multiple_of(rb * rows_blk, rows_blk), rows_blk), :] = acc
        return carry

    lax.fori_loop(0, seq // rows_blk, body, 0)


def _conformer_cols_conv(geom, proj, w, cb):
    seq = geom.seq
    nblk = D_CONF // LANES
    val0 = COL_VAL * D_CONF // LANES
    gate0 = COL_GATE * D_CONF // LANES
    return pl.pallas_call(
        functools.partial(_conf_cols_kernel, seq=seq),
        grid=(geom.bsz, nblk),
        in_specs=[pl.BlockSpec((seq, LANES), lambda b, c: (b, val0 + c)),
                  pl.BlockSpec((seq, LANES), lambda b, c: (b, gate0 + c)),
                  pl.BlockSpec((CONF_WIDTH, LANES), lambda b, c: (0, c)),
                  pl.BlockSpec((1, LANES), lambda b, c: (0, c))],
        out_specs=pl.BlockSpec((seq, LANES), lambda b, c: (b, c)),
        out_shape=jax.ShapeDtypeStruct((geom.n_lat, D_CONF), F32),
        scratch_shapes=[pltpu.VMEM((seq + 2 * CONF_HALF * GRID_W, LANES), F32)],
        compiler_params=_cparams(("arbitrary", "arbitrary")),
        name="conformer_cols",
    )(proj, proj, w, cb.reshape(1, D_CONF))


def _ln_swish_kernel(v_ref, g_ref, b_ref, o_ref):
    o_ref[...] = _silu(_standardize(v_ref[...]) * g_ref[...] + b_ref[...]).astype(o_ref.dtype)


def _ln_swish(conv, ln_g, ln_b):
    rows = conv.shape[0]
    full = pl.BlockSpec((1, D_CONF), lambda i: (0, 0))
    return pl.pallas_call(
        _ln_swish_kernel,
        grid=(rows // ROW_TILE,),
        in_specs=[pl.BlockSpec((ROW_TILE, D_CONF), lambda i: (i, 0)), full, full],
        out_specs=pl.BlockSpec((ROW_TILE, D_CONF), lambda i: (i, 0)),
        out_shape=jax.ShapeDtypeStruct((rows, D_CONF), BF16),
        compiler_params=_cparams(("arbitrary",)),
        name="conformer_ln",
    )(conv, ln_g.reshape(1, D_CONF), ln_b.reshape(1, D_CONF))


def _top2(logits):
    lane_i = lax.broadcasted_iota(jnp.int32, logits.shape, 1)
    lane = lane_i.astype(F32)
    m1 = jnp.max(logits, axis=-1, keepdims=True)
    i1 = jnp.min(jnp.where(logits == m1, lane, float(LANES)), axis=-1, keepdims=True)
    rest = jnp.where(lane == i1, -jnp.inf, logits)
    m2 = jnp.max(rest, axis=-1, keepdims=True)
    i2 = jnp.min(jnp.where(rest == m2, lane, float(LANES)), axis=-1, keepdims=True)
    e2 = jnp.exp(m2 - m1)
    w1 = 1.0 / (1.0 + e2)
    w2 = e2 / (1.0 + e2)
    idx = jnp.where(lane_i == 0, i1, jnp.where(lane_i == 1, i2, 0.0)).astype(jnp.int32)
    wts = jnp.where(lane_i == 0, w1, jnp.where(lane_i == 1, w2, 0.0))
    return idx, wts


HALF_D = D_MODEL // 2
_HI_MASK = 0xFFFF0000


def _pack_bf16_pairs(u):
    bits = pltpu.bitcast(u.astype(BF16).astype(F32), jnp.uint32)
    return (bits[:, HALF_D:] & jnp.uint32(_HI_MASK)) | (bits[:, :HALF_D] >> 16)


def _unpack_bf16_pairs(p):
    lo = pltpu.bitcast(p << 16, F32).astype(BF16)
    hi = pltpu.bitcast(p & jnp.uint32(_HI_MASK), F32).astype(BF16)
    return lo, hi


def _residual_norm(h, f, gate, ln_g, ln_b):
    return _standardize(ALPHA * h + gate * f) * ln_g + ln_b


def _epilogue_kernel(*refs, modulate, router, lat_tiles):
    if lat_tiles is not None:
        h_in = jnp.where(pl.program_id(0) < lat_tiles, refs[0][...], refs[1][...])
        refs = refs[1:]
    else:
        h_in = refs[0][...]
    _, f_ref, gate_ref, lg_ref, lb_ref = refs[:5]
    pos = 5
    if modulate:
        sh_ref, sc_ref = refs[pos:pos + 2]
        pos += 2
    if router:
        rwh_ref, rwl_ref, rb_ref = refs[pos:pos + 3]
        pos += 3
    outs = refs[pos:]
    hn = _residual_norm(h_in, f_ref[...], gate_ref[...], lg_ref[...], lb_ref[...])
    outs[0][...] = hn
    if modulate:
        u = _standardize(hn) * (1.0 + sc_ref[...]) + sh_ref[...]
        if router:
            outs[1][...] = _pack_bf16_pairs(u)
        else:
            outs[1][...] = u.astype(BF16)
    if router:
        u_hi = u.astype(BF16)
        u_lo = (u - u_hi.astype(F32)).astype(BF16)
        logits = (jnp.dot(u_hi, rwh_ref[...], preferred_element_type=F32)
                  + jnp.dot(u_lo, rwh_ref[...], preferred_element_type=F32)
                  + jnp.dot(u_hi, rwl_ref[...], preferred_element_type=F32)) + rb_ref[...]
        lane = lax.broadcasted_iota(jnp.int32, logits.shape, 1)
        logits = jnp.where(lane < N_EXPERTS, logits, -jnp.inf)
        idx, wts = _top2(logits)
        outs[2][...] = idx
        outs[3][...] = wts


def _epilogue(geom, h, f, rows, mods, gate_slot, ln_g, ln_b, next_mods=None, next_slots=None, router=None):
    n_tiles = rows // ROW_TILE
    row = pl.BlockSpec((ROW_TILE, D_MODEL), lambda i: (i, 0))
    vec = pl.BlockSpec((1, D_MODEL), lambda i: (0, 0))
    lat_tiles = None
    if isinstance(h, tuple):
        lat_tiles = geom.lat_tiles
        h_specs, h_args = _split_row_specs(lat_tiles), list(h)
    else:
        h_specs, h_args = [row], [h]
    in_specs = h_specs + [row, _mod_spec(geom, gate_slot), vec, vec]
    args = h_args + [f, mods, ln_g.reshape(1, D_MODEL), ln_b.reshape(1, D_MODEL)]
    out_specs = [row]
    out_shape = [jax.ShapeDtypeStruct((rows, D_MODEL), F32)]
    modulate = next_mods is not None
    if modulate:
        in_specs += [_mod_spec(geom, next_slots[0]), _mod_spec(geom, next_slots[1])]
        args += [next_mods, next_mods]
        if router is not None:
            out_specs.append(pl.BlockSpec((ROW_TILE, HALF_D), lambda i: (i, 0)))
            out_shape.append(jax.ShapeDtypeStruct((rows, HALF_D), jnp.uint32))
        else:
            out_specs.append(row)
            out_shape.append(jax.ShapeDtypeStruct((rows, D_MODEL), BF16))
    if router is not None:
        rw_hi, rw_lo, rb = router
        small = pl.BlockSpec((D_MODEL, LANES), lambda i: (0, 0))
        in_specs += [small, small, pl.BlockSpec((1, LANES), lambda i: (0, 0))]
        args += [rw_hi, rw_lo, rb]
        lane_blk = pl.BlockSpec((ROW_TILE, LANES), lambda i: (i, 0))
        out_specs += [lane_blk, lane_blk]
        out_shape += [jax.ShapeDtypeStruct((rows, LANES), jnp.int32),
                      jax.ShapeDtypeStruct((rows, LANES), F32)]
    return pl.pallas_call(
        functools.partial(_epilogue_kernel, modulate=modulate, router=router is not None, lat_tiles=lat_tiles),
        grid=(n_tiles,),
        in_specs=in_specs,
        out_specs=out_specs,
        out_shape=out_shape,
        compiler_params=_cparams(("arbitrary",)),
        name="epilogue",
    )(*args)


GATHER_ROWS = 256


def _row_copy(src_hbm, dst_vmem, sem, src_row, dst_row):
    return pltpu.make_async_copy(src_hbm.at[pl.ds(src_row, 1), :], dst_vmem.at[pl.ds(dst_row, 1), :], sem)


def _issue_rows(src_ref, dst_refs, sem, idx_fn, n_rows):
    def body(r, carry):
        for p, dst in enumerate(dst_refs):
            _row_copy(src_ref, dst, sem, idx_fn(len(dst_refs) * r + p), r).start(priority=p)
        return carry

    lax.fori_loop(0, n_rows, body, 0, unroll=8)


def _drain_rows(src_ref, dst_refs, sem, n_rows):
    for dst in dst_refs:
        pltpu.make_async_copy(src_ref.at[pl.ds(0, n_rows), :], dst, sem).wait()


def _double_buffered_rows(src_ref, bufs_of_slot, sem, cur_idx_ref, nxt_idx_ref, n_rows):
    i = pl.program_id(0)
    slot = i % 2

    @pl.when(i == 0)
    def _():
        _issue_rows(src_ref, bufs_of_slot(0), sem.at[0], lambda q: cur_idx_ref[0, 0, q], n_rows)

    @pl.when(i + 1 < pl.num_programs(0))
    def _():
        _issue_rows(src_ref, bufs_of_slot(1 - slot), sem.at[1 - slot], lambda q: nxt_idx_ref[0, 0, q], n_rows)

    _drain_rows(src_ref, bufs_of_slot(slot), sem.at[slot], n_rows)
    return slot


def _idx_specs(per_step, steps):
    cur = pl.BlockSpec((1, 1, per_step), lambda i: (i, 0, 0), memory_space=pltpu.SMEM)
    nxt = pl.BlockSpec((1, 1, per_step), lambda i: (jnp.minimum(i + 1, steps - 1), 0, 0), memory_space=pltpu.SMEM)
    return cur, nxt


def _gather_kernel(idx_ref, nxt_ref, src_ref, o_ref, buf_ref, sem):
    half = GATHER_ROWS // 2

    def bufs(slot):
        return (buf_ref.at[slot, 0], buf_ref.at[slot, 1])

    slot = _double_buffered_rows(src_ref, bufs, sem, idx_ref, nxt_ref, half)
    for p in range(2):
        lo, hi = _unpack_bf16_pairs(buf_ref[slot, p])
        o_ref[p, :, :HALF_D] = lo
        o_ref[p, :, HALF_D:] = hi


def _gather_rows(src, slot_token):
    n_slots = slot_token.shape[0]
    width = src.shape[1]
    steps = n_slots // GATHER_ROWS
    half = GATHER_ROWS // 2
    idx = jnp.transpose(slot_token.reshape(steps, 2, half), (0, 2, 1)).reshape(steps, 1, GATHER_ROWS)
    cur, nxt = _idx_specs(GATHER_ROWS, steps)
    out = pl.pallas_call(
        _gather_kernel,
        grid=(steps,),
        in_specs=[cur, nxt, pl.BlockSpec(memory_space=pl.ANY)],
        out_specs=pl.BlockSpec((None, 2, half, 2 * width), lambda i: (i, 0, 0, 0)),
        out_shape=jax.ShapeDtypeStruct((steps, 2, half, 2 * width), BF16),
        scratch_shapes=[pltpu.VMEM((2, 2, half, width), src.dtype), pltpu.SemaphoreType.DMA((2,))],
        compiler_params=_cparams(("arbitrary",)),
        name="moe_gather",
    )(idx, idx, src)
    return out.reshape(n_slots, 2 * width)


def _tile_idx(w, nused_ref):
    return jnp.minimum(w, nused_ref[0] - 1)


def _fresh_expert(te_ref, w):
    return (w == 0) | (te_ref[w] != te_ref[jnp.maximum(w - 1, 0)])


def _moe_up_kernel(te_ref, nused_ref, a_ref, w1_ref, w3_ref, o_ref, w1b_ref, w3b_ref):
    w = pl.program_id(1)
    used = w < nused_ref[0]

    @pl.when(used & _fresh_expert(te_ref, w))
    def _():
        w1b_ref[...] = w1_ref[...].astype(BF16)
        w3b_ref[...] = w3_ref[...].astype(BF16)

    @pl.when(used)
    def _():
        a = a_ref[...]
        h1 = jnp.dot(a, w1b_ref[...], preferred_element_type=F32)
        h3 = jnp.dot(a, w3b_ref[...], preferred_element_type=F32)
        o_ref[...] = (_silu(h1) * h3).astype(o_ref.dtype)

    @pl.when(jnp.logical_not(used))
    def _():
        o_ref[...] = jnp.zeros_like(o_ref)


MOE_UP_VMEM = 59 * 1024 * 1024


def _moe_up(a_sorted, w1, w3, tile_expert, n_used, tf=512):
    n_slots, k = a_sorted.shape
    n_tiles = n_slots // MOE_TM
    f = w1.shape[2]
    w_map = lambda j, w, te, nu: (te[_tile_idx(w, nu)], 0, j)
    grid_spec = pltpu.PrefetchScalarGridSpec(
        num_scalar_prefetch=2,
        grid=(f // tf, n_tiles),
        in_specs=[pl.BlockSpec((MOE_TM, k), lambda j, w, te, nu: (_tile_idx(w, nu), 0)),
                  pl.BlockSpec((None, k, tf), w_map),
                  pl.BlockSpec((None, k, tf), w_map)],
        out_specs=pl.BlockSpec((MOE_TM, tf), lambda j, w, te, nu: (w, j)),
        scratch_shapes=[pltpu.VMEM((k, tf), BF16), pltpu.VMEM((k, tf), BF16)],
    )
    return pl.pallas_call(
        _moe_up_kernel,
        grid_spec=grid_spec,
        out_shape=jax.ShapeDtypeStruct((n_slots, f), BF16),
        compiler_params=_cparams(("arbitrary", "arbitrary"), MOE_UP_VMEM),
        name="moe_up",
    )(tile_expert, n_used, a_sorted, w1, w3)


def _moe_down_kernel(te_ref, nused_ref, a_ref, w_ref, o_ref, wb_ref):
    w = pl.program_id(1)
    used = w < nused_ref[0]

    @pl.when(used & _fresh_expert(te_ref, w))
    def _():
        wb_ref[...] = w_ref[...].astype(BF16)

    @pl.when(used)
    def _():
        o_ref[...] = jnp.dot(a_ref[...], wb_ref[...], preferred_element_type=F32)

    @pl.when(jnp.logical_not(used))
    def _():
        o_ref[...] = jnp.zeros_like(o_ref)


def _moe_down(hid, w2, tile_expert, n_used, tn=1024):
    n_slots, k = hid.shape
    n_tiles = n_slots // MOE_TM
    n = w2.shape[2]
    grid_spec = pltpu.PrefetchScalarGridSpec(
        num_scalar_prefetch=2,
        grid=(n // tn, n_tiles),
        in_specs=[pl.BlockSpec((MOE_TM, k), lambda j, w, te, nu: (_tile_idx(w, nu), 0)),
                  pl.BlockSpec((None, k, tn), lambda j, w, te, nu: (te[_tile_idx(w, nu)], 0, j))],
        out_specs=pl.BlockSpec((MOE_TM, tn), lambda j, w, te, nu: (w, j)),
        scratch_shapes=[pltpu.VMEM((k, tn), BF16)],
    )
    return pl.pallas_call(
        _moe_down_kernel,
        grid_spec=grid_spec,
        out_shape=jax.ShapeDtypeStruct((n_slots, n), F32),
        compiler_params=_cparams(("arbitrary", "arbitrary")),
        name="moe_down",
    )(tile_expert, n_used, hid, w2)


COMBINE_ROWS = 128


def _combine_kernel(pos_ref, nxt_ref, h_ref, wts_ref, gate_ref, lg_ref, lb_ref, y_ref, o_ref, buf_ref, sem):
    def bufs(slot):
        return (buf_ref.at[slot, 0], buf_ref.at[slot, 1])

    slot = _double_buffered_rows(y_ref, bufs, sem, pos_ref, nxt_ref, COMBINE_ROWS)
    wts = wts_ref[...]
    f = buf_ref[slot, 0] * wts[:, 0:1] + buf_ref[slot, 1] * wts[:, 1:2]
    o_ref[...] = _residual_norm(h_ref[...], f, gate_ref[...], lg_ref[...], lb_ref[...])


def _moe_combine(geom, h, y_sorted, pos, wts, mods, gate_slot, ln_g, ln_b):
    rows = geom.n_lat
    steps = rows // COMBINE_ROWS
    per_mod = ROW_TILE // COMBINE_ROWS
    row = pl.BlockSpec((COMBINE_ROWS, D_MODEL), lambda i: (i, 0))
    vec = pl.BlockSpec((1, D_MODEL), lambda i: (0, 0))
    cur, nxt = _idx_specs(2 * COMBINE_ROWS, steps)
    pos3 = pos.reshape(steps, 1, 2 * COMBINE_ROWS)
    return pl.pallas_call(
        _combine_kernel,
        grid=(steps,),
        in_specs=[cur, nxt, row,
                  pl.BlockSpec((COMBINE_ROWS, LANES), lambda i: (i, 0)),
                  pl.BlockSpec((None, 1, D_MODEL), lambda i: (geom.mod_row(i // per_mod) * 6 + gate_slot, 0, 0)),
                  vec, vec,
                  pl.BlockSpec(memory_space=pl.ANY)],
        out_specs=row,
        out_shape=jax.ShapeDtypeStruct((rows, D_MODEL), F32),
        scratch_shapes=[pltpu.VMEM((2, 2, COMBINE_ROWS, D_MODEL), F32), pltpu.SemaphoreType.DMA((2,))],
        compiler_params=_cparams(("arbitrary",)),
        name="moe_combine",
    )(pos3, pos3, h, wts, mods, ln_g.reshape(1, D_MODEL), ln_b.reshape(1, D_MODEL), y_sorted)


def _route(top_idx, n_tokens):
    flat = top_idx.reshape(-1)
    onehot = (jnp.arange(N_EXPERTS, dtype=jnp.int32)[:, None] == flat[None, :]).astype(jnp.int32)
    blk = 2 * ROW_TILE
    inner = jnp.cumsum(onehot.reshape(N_EXPERTS, -1, blk), axis=2)
    blk_tot = inner[:, :, -1]
    blk_off = jnp.cumsum(blk_tot, axis=1) - blk_tot
    csum = (inner + blk_off[:, :, None]).reshape(N_EXPERTS, -1)
    rank = jnp.sum(csum * onehot, axis=0) - 1
    counts = csum[:, -1]
    tiles_per = (counts + MOE_TM - 1) // MOE_TM
    tile_end = jnp.cumsum(tiles_per)
    tile_start = tile_end - tiles_per
    pos = (jnp.sum(onehot * (tile_start * MOE_TM)[:, None], axis=0) + rank).astype(jnp.int32)
    n_tiles = (2 * n_tokens) // MOE_TM + N_EXPERTS
    n_used = tile_end[-1].astype(jnp.int32)
    tile_ids = jnp.arange(n_tiles, dtype=jnp.int32)
    tile_expert = jnp.sum((tile_ids[:, None] >= tile_end[None, :]).astype(jnp.int32), axis=1)
    tile_expert = jnp.minimum(tile_expert, N_EXPERTS - 1).astype(jnp.int32)
    slot_token = jnp.zeros((n_tiles * MOE_TM,), jnp.int32).at[pos].set(
        jnp.arange(2 * n_tokens, dtype=jnp.int32) // 2)
    return pos, slot_token, tile_expert, n_used.reshape(1)


def kernel(x, c, ctx, c_ctx, ada_w, ada_b, w_in, mamba_conv_w, mamba_conv_b, dt_bias_fwd, dt_bias_bwd, a_log_fwd, a_log_bwd, d_skip, ssm_norm_w, conf_conv_w, conf_conv_b, conf_ln_g, conf_ln_b, w_out, ln1_g, ln1_b, ln2_g, ln2_b, ffn_w1, ffn_w3, ffn_w2, router_w, router_b, moe_w1, moe_w3, moe_w2):
    bsz, seq, d = x.shape
    ctx_len = ctx.shape[1]
    geom = _Geom(bsz, seq, ctx_len)
    n_all, n_lat = geom.n_all, geom.n_lat

    mod_rows = SUBLANES
    cond_rows = jnp.zeros((mod_rows, d), F32).at[:bsz].set(c).at[bsz].set(c_ctx)
    mods_all = _ada_mods(cond_rows, ada_w, ada_b).reshape(DEPTH, mod_rows * 6, 1, d)

    h = (x.reshape(n_lat, d), ctx.reshape(bsz * ctx_len, d))
    u = _pack_modulate(geom, h[0], h[1], mods_all[0])

    tm_all = _pick_tile(n_all, 1056)
    tm_lat = _pick_tile(n_lat, 1024)

    w_main_all, w_dt_all = _w_in_prep(w_in)
    w_out_all = w_out.astype(BF16)

    for i in range(DEPTH):
        mods = mods_all[i]
        last = i == DEPTH - 1
        proj = _matmul([(u, w_main_all, (i, 0))], n_all, tm_all, 1024, F32, "in_proj")
        dtr = _matmul([(u, w_dt_all, (i, 0))], n_all, tm_all, LANES, F32, "dt_proj")
        dtt = jnp.transpose(dtr[:, :2 * SSM_HEADS])

        xs, bc = _ssm_conv(geom, proj, mamba_conv_w[i], mamba_conv_b[i])
        y_b = _ssd_pass(geom, xs, bc, dtr, dtt, dt_bias_bwd[i], a_log_bwd[i], rev=True)
        dsk = jnp.repeat(d_skip[i], SSM_HEADDIM).reshape(1, D_SSM)
        y_ssm = _ssd_pass(geom, xs, bc, dtr, dtt, dt_bias_fwd[i], a_log_fwd[i], rev=False,
                          final_args=(y_b, proj, dsk, ssm_norm_w[i].reshape(1, D_SSM)))

        if i % 2 == 0:
            v = _conformer_rows(geom, proj, conf_conv_w[i], conf_conv_b[i], conf_ln_g[i], conf_ln_b[i])
        else:
            conv = _conformer_cols_conv(geom, proj, conf_conv_w[i], conf_conv_b[i])
            v = _ln_swish(conv, conf_ln_g[i], conf_ln_b[i])

        rows = n_lat if last else n_all
        tm = tm_lat if last else tm_all
        mix = _matmul([(y_ssm, w_out_all, (i, 0)), (v, w_out_all, (i, 1))], rows, tm, 1024, F32, "out_proj")

        j = i // 2
        if i % 2 == 0:
            h, u2 = _epilogue(geom, h, mix, rows, mods, 2, ln1_g[i], ln1_b[i], mods, (3, 4))
            w2 = ffn_w2[j].astype(BF16)
            hid = _gated_up(u2, ffn_w1[j], ffn_w3[j], rows, _pick_tile(rows, 1408), 256, FFN_PAD)
            f = _matmul_ktiled(hid, w2, rows, tm, 2048, 1024, "ffn_down")
            if last:
                h = _epilogue(geom, h, f, rows, mods, 5, ln2_g[i], ln2_b[i])[0]
            else:
                h, u = _epilogue(geom, h, f, rows, mods, 5, ln2_g[i], ln2_b[i], mods_all[i + 1], (0, 1))
        else:
            rw = jnp.pad(router_w[j], ((0, 0), (0, LANES - N_EXPERTS)))
            rw_hi = rw.astype(BF16)
            rw_lo = (rw - rw_hi.astype(F32)).astype(BF16)
            rb = jnp.pad(router_b[j], (0, LANES - N_EXPERTS)).reshape(1, LANES)
            assert last, "a routed layer is only supported as the final layer"
            h, u2p, top_idx, top_w = _epilogue(geom, h, mix, rows, mods, 2, ln1_g[i], ln1_b[i], mods, (3, 4),
                                                  router=(rw_hi, rw_lo, rb))
            pos, slot_token, tile_expert, n_used = _route(top_idx[:, :2], rows)
            a_sorted = _gather_rows(u2p, slot_token)
            hid = _moe_up(a_sorted, moe_w1[j], moe_w3[j], tile_expert, n_used)
            y_sorted = _moe_down(hid, moe_w2[j], tile_expert, n_used)
            h = _moe_combine(geom, h, y_sorted, pos, top_w, mods, 5, ln2_g[i], ln2_b[i])

    return h[:n_lat].reshape(bsz, seq, d)
```
